```python
import jax, jax.numpy as jnp
from jax import lax
import numpy as np

D_MODEL = 1024
BATCH = 8
SEQ = 8192
DEPTH = 2

N_META = 16
CHUNK = 64
META_PAD = (-N_META) % CHUNK
DN_HEADS = 8
DN_DK = 128
DN_DV = 128
DN_QK = DN_HEADS * DN_DK
DN_VW = DN_HEADS * DN_DV
DN_CONV = 5
SC_WIDTH = 1024
SC_CONV = 3
D_FF = -(-8 * D_MODEL // (3 * 256)) * 256
RMS_EPS = 1e-6
L2_EPS = 1e-6

SPLIT_SIZES = [DN_QK, DN_QK, DN_VW, DN_VW, 2 * DN_HEADS, 2 * DN_HEADS,
               SC_WIDTH, SC_WIDTH, SC_WIDTH, D_MODEL, D_MODEL]
W_IN_COLS = sum(SPLIT_SIZES)
SPLIT_POINTS = list(np.cumsum(SPLIT_SIZES)[:-1].tolist())

kernel_name = "hybrid_deltanet_shortconv_encoder"


def rmsnorm(x, w):
    xf = x.astype(jnp.float32)
    y = xf * lax.rsqrt(jnp.mean(xf * xf, axis=-1, keepdims=True) + RMS_EPS)
    return (y * w.astype(jnp.float32)).astype(x.dtype)


def l2norm(t):
    return t * lax.rsqrt(jnp.sum(t * t, axis=-1, keepdims=True) + L2_EPS)


def depthwise_conv_centred(x, w):
    K, C = w.shape
    r = K // 2
    return lax.conv_general_dilated(
        x, w[:, None, :].astype(x.dtype), window_strides=(1,), padding=[(r, r)],
        dimension_numbers=("NWC", "WIO", "NWC"), feature_group_count=C)


def chunk_gated_delta_rule(q, k, v, g, beta):
    Bsz, T, H, dk = q.shape
    dv = v.shape[-1]
    N = T // CHUNK

    def chunks(t):
        return t.reshape(Bsz, N, CHUNK, H, t.shape[-1]).transpose(0, 1, 3, 2, 4)

    q, k, v = chunks(q), chunks(k), chunks(v)
    g = g.reshape(Bsz, N, CHUNK, H).transpose(0, 1, 3, 2)
    beta = beta.reshape(Bsz, N, CHUNK, H).transpose(0, 1, 3, 2)
    g = jnp.cumsum(g, axis=-1)

    tri = jnp.tril(jnp.ones((CHUNK, CHUNK), dtype=bool))
    strict = jnp.tril(jnp.ones((CHUNK, CHUNK), dtype=bool), -1)
    decay = jnp.exp(jnp.where(tri, g[..., :, None] - g[..., None, :], -jnp.inf))

    kk = jnp.einsum("bnhid,bnhjd->bnhij", k, k)
    lmat = jnp.where(strict, beta[..., :, None] * kk * decay, 0.0)
    a = lmat + jnp.eye(CHUNK, dtype=lmat.dtype)
    rhs = jnp.concatenate([v * beta[..., None], k * (beta * jnp.exp(g))[..., None]], axis=-1)
    sol = lax.linalg.triangular_solve(a, rhs, left_side=True, lower=True, unit_diagonal=True)
    u, w = sol[..., :dv], sol[..., dv:]

    qk = jnp.einsum("bnhid,bnhjd->bnhij", q, k) * decay
    q_dec = q * jnp.exp(g)[..., None]
    g_last = g[..., -1]
    k_dec = k * jnp.exp(g_last[..., None] - g)[..., None]

    xs = tuple(jnp.moveaxis(t, 1, 0) for t in (qk, q_dec, k_dec, u, w, jnp.exp(g_last)))

    def step(S, inp):
        qk_c, qd_c, kd_c, u_c, w_c, gl_c = inp
        v_new = u_c - jnp.einsum("bhcd,bhde->bhce", w_c, S)
        o_c = (jnp.einsum("bhcd,bhde->bhce", qd_c, S)
               + jnp.einsum("bhij,bhje->bhie", qk_c, v_new))
        S = S * gl_c[..., None, None] + jnp.einsum("bhcd,bhce->bhde", kd_c, v_new)
        return S, o_c

    S0 = jnp.zeros((Bsz, H, dk, dv), jnp.float32)
    _, o = lax.scan(step, S0, xs)
    return o.transpose(1, 0, 3, 2, 4).reshape(Bsz, T, H, dv)


def gated_deltanet_bidir(q, k, v, z, b_raw, a_raw, conv_w, A_log, dt_bias, norm_w):
    Bsz, L, _ = q.shape
    f32 = jnp.float32
    qkv = jax.nn.silu(depthwise_conv_centred(jnp.concatenate([q, k, v], axis=-1), conv_w)).astype(f32)
    qc, kc, vc = jnp.split(qkv, [DN_QK, 2 * DN_QK], axis=-1)
    qc = l2norm(qc.reshape(Bsz, L, DN_HEADS, DN_DK)) * (DN_DK ** -0.5)
    kc = l2norm(kc.reshape(Bsz, L, DN_HEADS, DN_DK))
    vc = vc.reshape(Bsz, L, DN_HEADS, DN_DV)
    beta = jax.nn.sigmoid(b_raw.astype(f32)).reshape(Bsz, L, 2, DN_HEADS)
    g = -jnp.exp(A_log.astype(f32)) * jax.nn.softplus(
        a_raw.astype(f32).reshape(Bsz, L, 2, DN_HEADS) + dt_bias.astype(f32))

    def pad(t):
        return jnp.pad(t, [(0, 0), (META_PAD, 0)] + [(0, 0)] * (t.ndim - 2))

    qp, kp, vp, bp, gp = pad(qc), pad(kc), pad(vc), pad(beta), pad(g)
    o_fwd = chunk_gated_delta_rule(qp, kp, vp, gp[:, :, 0], bp[:, :, 0])

    def flip(t):
        return jnp.flip(t, axis=1)

    o_bwd = flip(chunk_gated_delta_rule(flip(qp), flip(kp), flip(vp), flip(gp[:, :, 1]), flip(bp[:, :, 1])))
    o = (o_fwd + o_bwd)[:, META_PAD:]
    o = o * lax.rsqrt(jnp.mean(o * o, axis=-1, keepdims=True) + RMS_EPS) * norm_w.astype(f32)
    o = o * jax.nn.silu(z.reshape(Bsz, L, DN_HEADS, DN_DV).astype(f32))
    return o.reshape(Bsz, L, DN_VW).astype(z.dtype)


def setup_inputs(seed: int = 0) -> dict:
    key = jax.random.key(seed)
    ks = jax.random.split(key, 20)
    f32 = jnp.float32

    def nrm(k, shape, scale):
        return jax.random.normal(k, shape, f32) * scale

    x = jax.random.normal(ks[0], (BATCH, SEQ, D_MODEL), f32)
    meta_tokens = nrm(ks[1], (N_META, D_MODEL), 1.0)
    norm1_w = 1.0 + nrm(ks[2], (DEPTH, D_MODEL), 0.02)
    w_in = nrm(ks[3], (DEPTH, D_MODEL, W_IN_COLS), D_MODEL ** -0.5)
    dn_conv_w = nrm(ks[4], (DEPTH, DN_CONV, DN_QK * 2 + DN_VW), DN_CONV ** -0.5)
    A_log = jnp.log(jax.random.uniform(ks[5], (DEPTH, 2, DN_HEADS), f32, 1.0, 16.0))
    dt = jnp.exp(jax.random.uniform(ks[6], (DEPTH, 2, DN_HEADS), f32, np.log(1e-3), np.log(1e-1)))
    dt_bias = dt + jnp.log(-jnp.expm1(-dt))
    dn_norm_w = 1.0 + nrm(ks[7], (DEPTH, DN_DV), 0.02)
    sc_conv_w = nrm(ks[8], (DEPTH, SC_CONV, SC_WIDTH), SC_CONV ** -0.5)
    w_branch_dn = nrm(ks[9], (DEPTH, DN_VW, D_MODEL), DN_VW ** -0.5)
    w_branch_sc = nrm(ks[10], (DEPTH, SC_WIDTH, D_MODEL), SC_WIDTH ** -0.5)
    w_out = nrm(ks[11], (DEPTH, D_MODEL, D_MODEL), D_MODEL ** -0.5)
    norm2_w = 1.0 + nrm(ks[12], (DEPTH, D_MODEL), 0.02)
    w_gate_up = nrm(ks[13], (DEPTH, D_MODEL, 2 * D_FF), D_MODEL ** -0.5)
    w_down = nrm(ks[14], (DEPTH, D_FF, D_MODEL), D_FF ** -0.5)
    final_norm_w = 1.0 + nrm(ks[15], (D_MODEL,), 0.02)
    return {"x": x, "meta_tokens": meta_tokens, "norm1_w": norm1_w, "w_in": w_in,
            "dn_conv_w": dn_conv_w, "A_log": A_log, "dt_bias": dt_bias, "dn_norm_w": dn_norm_w,
            "sc_conv_w": sc_conv_w, "w_branch_dn": w_branch_dn, "w_branch_sc": w_branch_sc,
            "w_out": w_out, "norm2_w": norm2_w, "w_gate_up": w_gate_up, "w_down": w_down,
            "final_norm_w": final_norm_w}


def reference(x, meta_tokens, norm1_w, w_in, dn_conv_w, A_log, dt_bias, dn_norm_w,
              sc_conv_w, w_branch_dn, w_branch_sc, w_out, norm2_w, w_gate_up, w_down,
              final_norm_w):
    Bsz = x.shape[0]
    meta = jnp.broadcast_to(meta_tokens[None].astype(x.dtype), (Bsz, N_META, x.shape[-1]))
    h_res = jnp.concatenate([meta, x], axis=1)
    for l in range(DEPTH):
        h = rmsnorm(h_res, norm1_w[l])
        proj = h @ w_in[l]
        (q, k, v, z, b_raw, a_raw, sc_b, sc_c, sc_x, gate_a, gate_b) = jnp.split(proj, SPLIT_POINTS, axis=-1)
        o_dn = gated_deltanet_bidir(q, k, v, z, b_raw, a_raw, dn_conv_w[l], A_log[l], dt_bias[l], dn_norm_w[l])
        y_a = o_dn @ w_branch_dn[l]
        y_b = (sc_b * depthwise_conv_centred(sc_c * sc_x, sc_conv_w[l])) @ w_branch_sc[l]
        merged = jax.nn.sigmoid(gate_a) * y_a + jax.nn.sigmoid(gate_b) * y_b
        h_res = h_res + merged @ w_out[l]
        h2 = rmsnorm(h_res, norm2_w[l])
        gu = h2 @ w_gate_up[l]
        h_res = h_res + (jax.nn.silu(gu[..., :D_FF]) * gu[..., D_FF:]) @ w_down[l]
    out = rmsnorm(h_res, final_norm_w)
    return out[:, N_META:]
```

```python
import functools

import jax
import jax.numpy as jnp
from jax import lax
from jax.experimental import pallas as pl
from jax.experimental.pallas import tpu as pltpu

F32 = jnp.float32
BF16 = jnp.bfloat16

D_MODEL = 1024
N_META = 16
CHUNK = 64
PAD_FRONT = (-N_META) % CHUNK
HEADS = 8
HEAD_DIM = 128
DN_CONV = 5
SC_CONV = 3
D_FF = 2816
RMS_EPS = 1e-6
L2_EPS = 1e-6

BF16_SUBLANES = 16
VMEM_LIMIT = 56 * 1024 * 1024


def _silu(x):
    return x * jax.nn.sigmoid(x)


def _rms(x, w):
    return x * lax.rsqrt(jnp.mean(x * x, axis=-1, keepdims=True) + RMS_EPS) * w


def _dot(a, b):
    return jnp.dot(a, b, preferred_element_type=F32)


def _inproj_kernel(x_ref, nw_ref, w_ref, wba_ref, p_ref, ba_ref, h_ref):
    @pl.when(pl.program_id(1) == 0)
    def _():
        h = _rms(x_ref[...], nw_ref[...]).astype(BF16)
        h_ref[...] = h
        ba_ref[...] = _dot(h, wba_ref[...])

    p_ref[...] = _dot(h_ref[...], w_ref[...]).astype(BF16)


def _inproj(x, nw, w, wba, tm, tn):
    rows, n = x.shape[0], w.shape[1]
    return pl.pallas_call(
        _inproj_kernel,
        grid=(rows // tm, n // tn),
        in_specs=[
            pl.BlockSpec((tm, D_MODEL), lambda i, j: (i, 0)),
            pl.BlockSpec((1, D_MODEL), lambda i, j: (0, 0)),
            pl.BlockSpec((D_MODEL, tn), lambda i, j: (0, j)),
            pl.BlockSpec((D_MODEL, 256), lambda i, j: (0, 0)),
        ],
        out_specs=[
            pl.BlockSpec((tm, tn), lambda i, j: (i, j)),
            pl.BlockSpec((tm, 256), lambda i, j: (i, 0)),
        ],
        out_shape=[
            jax.ShapeDtypeStruct((rows, n), BF16),
            jax.ShapeDtypeStruct((rows, 256), F32),
        ],
        scratch_shapes=[pltpu.VMEM((tm, D_MODEL), BF16)],
        compiler_params=pltpu.CompilerParams(
            dimension_semantics=("parallel", "arbitrary"), vmem_limit_bytes=VMEM_LIMIT),
        name="inproj",
    )(x, nw, w, wba)


def _dn_kernel(nb, nc, pm_ref, pp_ref, pn_ref, ba_ref, cw_ref, gp_ref, o_ref, s_ref, xb_ref):
    b = pl.program_id(0)
    d = pl.program_id(1)
    n = pl.program_id(2)
    c = n + d * (nc - 1 - 2 * n)

    @pl.when(n == 0)
    def _():
        s_ref[...] = jnp.zeros_like(s_ref)

    is_last = jnp.logical_and(b == nb - 1, c == nc - 1)
    hb = BF16_SUBLANES
    xb_ref[0:hb, :] = pp_ref[...].astype(F32)
    xb_ref[hb:hb + CHUNK, :] = pm_ref[...].astype(F32)
    xb_ref[hb + CHUNK:, :] = jnp.where(is_last, 0.0, pn_ref[...].astype(F32))
    r = DN_CONV // 2
    acc = cw_ref[0:1, :] * xb_ref[hb - r:hb - r + CHUNK, :]
    for s in range(1, DN_CONV):
        acc = acc + cw_ref[s:s + 1, :] * xb_ref[hb - r + s:hb - r + s + CHUNK, :]
    qkv = _silu(acc)

    rid = lax.broadcasted_iota(jnp.int32, (CHUNK, 1), 0)
    valid = jnp.logical_or(c > 0, rid >= PAD_FRONT).astype(F32)

    ba = ba_ref[...]
    lane = lax.broadcasted_iota(jnp.int32, (1, 128), 1)
    a_lanes = jnp.logical_and(lane >= HEADS, lane < 2 * HEADS)
    neg_a = jnp.where(a_lanes, -jnp.exp(gp_ref[0, 0:1, :]), 0.0)
    g = neg_a * jax.nn.softplus(ba + gp_ref[0, 1:2, :]) * valid
    beta = jax.nn.sigmoid(ba)

    ri = lax.broadcasted_iota(jnp.int32, (CHUNK, CHUNK), 0)
    ci = lax.broadcasted_iota(jnp.int32, (CHUNK, CHUNK), 1)
    dij = (ri - ci) * (1 - 2 * d)
    tri = dij >= 0
    strict = dij > 0
    gc = jnp.dot(tri.astype(F32), g, preferred_element_type=F32, precision=lax.Precision.HIGHEST)
    gl = jnp.sum(g, axis=0, keepdims=True)
    gc_t = gc.T
    eg = jnp.exp(gc)
    egl = jnp.exp(gl - gc)
    eg_last = jnp.exp(gl)

    for h in range(HEADS):
        sl = slice(h * HEAD_DIM, (h + 1) * HEAD_DIM)
        qh = qkv[:, sl]
        kh = qkv[:, D_MODEL + h * HEAD_DIM:D_MODEL + (h + 1) * HEAD_DIM]
        vh = qkv[:, 2 * D_MODEL + h * HEAD_DIM:2 * D_MODEL + (h + 1) * HEAD_DIM] * valid
        qh = qh * lax.rsqrt(jnp.sum(qh * qh, axis=-1, keepdims=True) + L2_EPS) * (HEAD_DIM ** -0.5) * valid
        kh = kh * lax.rsqrt(jnp.sum(kh * kh, axis=-1, keepdims=True) + L2_EPS) * valid

        ga = HEADS + h
        g_col = gc[:, ga:ga + 1]
        g_row = gc_t[ga:ga + 1, :]
        decay = jnp.exp(jnp.where(tri, g_col - g_row, -jnp.inf))
        b_col = beta[:, h:h + 1]
        eg_col = eg[:, ga:ga + 1]

        kb = kh.astype(BF16)
        kq = jnp.concatenate([kb, qh.astype(BF16)], axis=0)
        kkqk = lax.dot_general(kq, kb, (((1,), (1,)), ((), ())), preferred_element_type=F32)
        kk = kkqk[:CHUNK]
        qk = kkqk[CHUNK:]

        nm = jnp.where(strict, -(b_col * kk * decay), 0.0)
        y = nm
        nb16 = nm.astype(BF16)
        pw = _dot(nb16, nb16)
        for j in range(5):
            pb = pw.astype(BF16)
            if j < 4:
                x2 = _dot(jnp.concatenate([pb, y.astype(BF16)], axis=0), pb)
                y = y + pw + x2[CHUNK:]
                pw = x2[:CHUNK]
            else:
                y = y + pw + _dot(y.astype(BF16), pb)

        rhs = jnp.concatenate([vh * b_col, kh * (b_col * eg_col)], axis=1)
        sol = rhs + _dot(y.astype(BF16), rhs.astype(BF16))
        u = sol[:, :HEAD_DIM]
        w = sol[:, HEAD_DIM:]

        s_old = s_ref[h]
        sb = s_old.astype(BF16)
        v_new = u - _dot(w.astype(BF16), sb)
        vb = v_new.astype(BF16)
        o = _dot((qh * eg_col).astype(BF16), sb) + _dot((qk * decay).astype(BF16), vb)
        kd = (kh * egl[:, ga:ga + 1]).astype(BF16)
        s_ref[h] = s_old * eg_last[:, ga:ga + 1] + lax.dot_general(
            kd, vb, (((0,), (0,)), ((), ())), preferred_element_type=F32)
        o_ref[0, :, sl] = o.astype(BF16)


def _deltanet(p, ba, cw, gp, nb, nc):
    rows = p.shape[0]
    hb = BF16_SUBLANES
    per = CHUNK // hb
    n_halo = rows // hb

    def blk(b, d, n):
        return b * nc + n + d * (nc - 1 - 2 * n)

    return pl.pallas_call(
        functools.partial(_dn_kernel, nb, nc),
        grid=(nb, 2, nc),
        in_specs=[
            pl.BlockSpec((CHUNK, 3 * D_MODEL), lambda b, d, n: (blk(b, d, n), 0)),
            pl.BlockSpec((hb, 3 * D_MODEL), lambda b, d, n: (jnp.maximum(blk(b, d, n) * per - 1, 0), 0)),
            pl.BlockSpec((hb, 3 * D_MODEL), lambda b, d, n: (jnp.minimum(blk(b, d, n) * per + per, n_halo - 1), 0)),
            pl.BlockSpec((CHUNK, 128), lambda b, d, n: (blk(b, d, n), d)),
            pl.BlockSpec((8, 3 * D_MODEL), lambda b, d, n: (0, 0)),
            pl.BlockSpec((1, 8, 128), lambda b, d, n: (d, 0, 0)),
        ],
        out_specs=pl.BlockSpec((1, CHUNK, D_MODEL), lambda b, d, n: (d, blk(b, d, n), 0)),
        out_shape=jax.ShapeDtypeStruct((2, rows, D_MODEL), BF16),
        scratch_shapes=[
            pltpu.VMEM((HEADS, HEAD_DIM, HEAD_DIM), F32),
            pltpu.VMEM((CHUNK + 2 * hb, 3 * D_MODEL), F32),
        ],
        compiler_params=pltpu.CompilerParams(
            dimension_semantics=("parallel", "arbitrary", "arbitrary"), vmem_limit_bytes=VMEM_LIMIT),
        name="deltanet",
    )(p, p, p, ba, cw, gp)


def _merge_kernel(nt, tm, o_ref, z_ref, scb_ref, scc_ref, scx_ref, sccp_ref, scxp_ref, sccn_ref, scxn_ref,
                  ga_ref, gb_ref, h_ref, nw_ref, cw_ref, wdn_ref, wsc_ref, wout_ref, out_ref, xb_ref):
    is_last = pl.program_id(0) == nt - 1

    o = o_ref[0].astype(F32) + o_ref[1].astype(F32)
    z = z_ref[...].astype(F32)
    parts = []
    for h in range(HEADS):
        sl = slice(h * HEAD_DIM, (h + 1) * HEAD_DIM)
        parts.append((_rms(o[:, sl], nw_ref[...]) * _silu(z[:, sl])).astype(BF16))
    y_a = _dot(jnp.concatenate(parts, axis=1), wdn_ref[...])

    cx = scc_ref[...].astype(F32) * scx_ref[...].astype(F32)
    hb = BF16_SUBLANES
    prev = sccp_ref[...].astype(F32) * scxp_ref[...].astype(F32)
    nxt = sccn_ref[...].astype(F32) * scxn_ref[...].astype(F32)
    xb_ref[8:8 + tm, :] = cx
    xb_ref[7:8, :] = prev[hb - 1:hb, :]
    xb_ref[8 + tm:9 + tm, :] = jnp.where(is_last, 0.0, nxt[0:1, :])
    conv = cw_ref[0:1, :] * xb_ref[7:7 + tm, :] + cw_ref[1:2, :] * cx + cw_ref[2:3, :] * xb_ref[9:9 + tm, :]
    y_b = _dot((scb_ref[...].astype(F32) * conv).astype(BF16), wsc_ref[...])

    merged = (jax.nn.sigmoid(ga_ref[...].astype(F32)) * y_a
              + jax.nn.sigmoid(gb_ref[...].astype(F32)) * y_b)
    out_ref[...] = h_ref[...] + _dot(merged.astype(BF16), wout_ref[...])


def _merge(o, p, hres, nw, cw, wdn, wsc, wout, tm):
    rows = hres.shape[0]
    nt = rows // tm
    hb = BF16_SUBLANES
    per = tm // hb
    n_halo = rows // hb

    def col(c):
        return pl.BlockSpec((tm, D_MODEL), lambda i: (i, c))

    def prev(c):
        return pl.BlockSpec((hb, D_MODEL), lambda i: (jnp.maximum(i * per - 1, 0), c))

    def nxt(c):
        return pl.BlockSpec((hb, D_MODEL), lambda i: (jnp.minimum(i * per + per, n_halo - 1), c))

    def full(shape):
        return pl.BlockSpec(shape, lambda i: (0,) * len(shape))

    return pl.pallas_call(
        functools.partial(_merge_kernel, nt, tm),
        grid=(nt,),
        in_specs=[
            pl.BlockSpec((2, tm, D_MODEL), lambda i: (0, i, 0)),
            col(3), col(4), col(5), col(6), prev(5), prev(6), nxt(5), nxt(6), col(7), col(8),
            pl.BlockSpec((tm, D_MODEL), lambda i: (i, 0)),
            full((1, HEAD_DIM)), full((8, D_MODEL)),
            full((D_MODEL, D_MODEL)), full((D_MODEL, D_MODEL)), full((D_MODEL, D_MODEL)),
        ],
        out_specs=pl.BlockSpec((tm, D_MODEL), lambda i: (i, 0)),
        out_shape=jax.ShapeDtypeStruct((rows, D_MODEL), F32),
        scratch_shapes=[pltpu.VMEM((tm + 16, D_MODEL), F32)],
        input_output_aliases={11: 0},
        compiler_params=pltpu.CompilerParams(
            dimension_semantics=("parallel",), vmem_limit_bytes=VMEM_LIMIT),
        name="merge",
    )(o, p, p, p, p, p, p, p, p, p, p, hres, nw, cw, wdn, wsc, wout)


def _mlp_kernel(final, x_ref, nw_ref, wg_ref, wu_ref, wd_ref, fw_ref, out_ref):
    x = x_ref[...]
    h2 = _rms(x, nw_ref[...]).astype(BF16)
    acc = x
    half = D_FF // 2
    for f in range(2):
        sl = slice(f * half, (f + 1) * half)
        a = _silu(_dot(h2, wg_ref[:, sl])) * _dot(h2, wu_ref[:, sl])
        acc = acc + _dot(a.astype(BF16), wd_ref[sl, :])
    if final:
        acc = _rms(acc, fw_ref[...])
    out_ref[...] = acc


def _mlp(x, nw, wg, wu, wd, fw, final, tm):
    rows = x.shape[0]

    def full(shape):
        return pl.BlockSpec(shape, lambda i: (0,) * len(shape))

    return pl.pallas_call(
        functools.partial(_mlp_kernel, final),
        grid=(rows // tm,),
        in_specs=[
            pl.BlockSpec((tm, D_MODEL), lambda i: (i, 0)),
            full((1, D_MODEL)), full((D_MODEL, D_FF)), full((D_MODEL, D_FF)), full((D_FF, D_MODEL)),
            full((1, D_MODEL)),
        ],
        out_specs=pl.BlockSpec((tm, D_MODEL), lambda i: (i, 0)),
        out_shape=jax.ShapeDtypeStruct((rows, D_MODEL), F32),
        input_output_aliases={0: 0},
        compiler_params=pltpu.CompilerParams(
            dimension_semantics=("parallel",), vmem_limit_bytes=VMEM_LIMIT),
        name="mlp",
    )(x, nw, wg, wu, wd, fw)


def _row_tile(rows, want):
    t = want - want % BF16_SUBLANES
    while rows % t:
        t -= BF16_SUBLANES
    return t


def _pad_rows(a, n):
    return jnp.pad(a, ((0, n - a.shape[0]), (0, 0)))


def kernel(x, meta_tokens, norm1_w, w_in, dn_conv_w, A_log, dt_bias, dn_norm_w, sc_conv_w, w_branch_dn,
           w_branch_sc, w_out, norm2_w, w_gate_up, w_down, final_norm_w):
    nb, seq, _ = x.shape
    depth = w_in.shape[0]
    t_pad = PAD_FRONT + N_META + seq
    nc = t_pad // CHUNK
    rows = nb * t_pad

    meta = jnp.broadcast_to(meta_tokens[None].astype(x.dtype), (nb, N_META, D_MODEL))
    hres = jnp.concatenate([jnp.zeros((nb, PAD_FRONT, D_MODEL), x.dtype), meta, x], axis=1)
    hres = hres.reshape(rows, D_MODEL)

    q_cols = 4 * D_MODEL
    ba_cols = 4 * HEADS
    tm_in = _row_tile(rows, 768)
    tm_merge = _row_tile(rows, 256)
    tm_mlp = _row_tile(rows, 512)

    for l in range(depth):
        w_main = jnp.concatenate([w_in[l, :, :q_cols], w_in[l, :, q_cols + ba_cols:]], axis=1).astype(BF16)
        w_b = w_in[l, :, q_cols:q_cols + 2 * HEADS]
        w_a = w_in[l, :, q_cols + 2 * HEADS:q_cols + ba_cols]
        zpad = jnp.zeros((D_MODEL, 128 - 2 * HEADS), w_in.dtype)
        w_ba = jnp.concatenate([w_b[:, :HEADS], w_a[:, :HEADS], zpad,
                                w_b[:, HEADS:], w_a[:, HEADS:], zpad], axis=1).astype(BF16)
        p, ba = _inproj(hres, norm1_w[l][None], w_main, w_ba, tm_in, 1536)

        lane_pad = ((0, 0), (HEADS, 128 - 2 * HEADS))
        gp = jnp.stack([jnp.pad(A_log[l], lane_pad), jnp.pad(dt_bias[l], lane_pad)], axis=1)
        gp = jnp.pad(gp, ((0, 0), (0, 6), (0, 0))).astype(F32)
        o = _deltanet(p, ba, _pad_rows(dn_conv_w[l], 8), gp, nb, nc)

        hres = _merge(o, p, hres, dn_norm_w[l][None], _pad_rows(sc_conv_w[l], 8),
                      w_branch_dn[l].astype(BF16), w_branch_sc[l].astype(BF16), w_out[l].astype(BF16), tm_merge)

        hres = _mlp(hres, norm2_w[l][None], w_gate_up[l, :, :D_FF].astype(BF16),
                    w_gate_up[l, :, D_FF:].astype(BF16), w_down[l].astype(BF16), final_norm_w[None],
                    l == depth - 1, tm_mlp)

    return hres.reshape(nb, t_pad, D_MODEL)[:, PAD_FRONT + N_META:]
```

```python
import functools

import jax
import jax.numpy as jnp
from jax import lax
from jax.experimental import pallas as pl
from jax.experimental.pallas import tpu as pltpu

F32 = jnp.float32
BF16 = jnp.bfloat16

D_MODEL = 1024
N_META = 16
CHUNK = 64
PAD_FRONT = (-N_META) % CHUNK
HEADS = 8
HEAD_DIM = 128
DN_CONV = 5
SC_CONV = 3
D_FF = 2816
RMS_EPS = 1e-6
L2_EPS = 1e-6

BF16_SUBLANES = 16
VMEM_LIMIT = 56 * 1024 * 1024


def _silu(x):
    return x * jax.nn.sigmoid(x)


def _rms(x, w):
    return x * lax.rsqrt(jnp.mean(x * x, axis=-1, keepdims=True) + RMS_EPS) * w


def _dot(a, b):
    return jnp.dot(a, b, preferred_element_type=F32)


def _inproj_kernel(x_ref, nw_ref, w_ref, wba_ref, p_ref, ba_ref, h_ref):
    @pl.when(pl.program_id(1) == 0)
    def _():
        h = _rms(x_ref[...], nw_ref[...]).astype(BF16)
        h_ref[...] = h
        ba_ref[...] = _dot(h, wba_ref[...])

    p_ref[...] = _dot(h_ref[...], w_ref[...]).astype(BF16)


def _inproj(x, nw, w, wba, tm, tn):
    rows, n = x.shape[0], w.shape[1]
    return pl.pallas_call(
        _inproj_kernel,
        grid=(rows // tm, n // tn),
        in_specs=[
            pl.BlockSpec((tm, D_MODEL), lambda i, j: (i, 0)),
            pl.BlockSpec((1, D_MODEL), lambda i, j: (0, 0)),
            pl.BlockSpec((D_MODEL, tn), lambda i, j: (0, j)),
            pl.BlockSpec((D_MODEL, 256), lambda i, j: (0, 0)),
        ],
        out_specs=[
            pl.BlockSpec((tm, tn), lambda i, j: (i, j)),
            pl.BlockSpec((tm, 256), lambda i, j: (i, 0)),
        ],
        out_shape=[
            jax.ShapeDtypeStruct((rows, n), BF16),
            jax.ShapeDtypeStruct((rows, 256), F32),
        ],
        scratch_shapes=[pltpu.VMEM((tm, D_MODEL), BF16)],
        compiler_params=pltpu.CompilerParams(
            dimension_semantics=("parallel", "arbitrary"), vmem_limit_bytes=VMEM_LIMIT),
        name="inproj",
    )(x, nw, w, wba)


def _prep_kernel(nt, tm, t_pad, pm_ref, pp_ref, pn_ref, cw_ref, o_ref, xb_ref):
    i = pl.program_id(0)
    hb = BF16_SUBLANES
    xb_ref[0:hb, :] = pp_ref[...].astype(F32)
    xb_ref[hb:hb + tm, :] = pm_ref[...].astype(F32)
    xb_ref[hb + tm:, :] = jnp.where(i == nt - 1, 0.0, pn_ref[...].astype(F32))
    r = DN_CONV // 2
    acc = cw_ref[0:1, :] * xb_ref[hb - r:hb - r + tm, :]
    for s in range(1, DN_CONV):
        acc = acc + cw_ref[s:s + 1, :] * xb_ref[hb - r + s:hb - r + s + tm, :]
    qkv = _silu(acc)

    pos = lax.rem(i * tm, t_pad) + lax.broadcasted_iota(jnp.int32, (tm, 1), 0)
    pos = jnp.where(pos >= t_pad, pos - t_pad, pos)
    valid = (pos >= PAD_FRONT).astype(F32)

    for h in range(HEADS):
        for part in range(3):
            sl = slice(part * D_MODEL + h * HEAD_DIM, part * D_MODEL + (h + 1) * HEAD_DIM)
            t = qkv[:, sl]
            if part == 0:
                t = t * lax.rsqrt(jnp.sum(t * t, axis=-1, keepdims=True) + L2_EPS) * (HEAD_DIM ** -0.5)
            elif part == 1:
                t = t * lax.rsqrt(jnp.sum(t * t, axis=-1, keepdims=True) + L2_EPS)
            o_ref[:, sl] = (t * valid).astype(BF16)


def _prep(p, cw, tm, t_pad):
    rows = p.shape[0]
    nt = rows // tm
    hb = BF16_SUBLANES
    per = tm // hb
    n_halo = rows // hb
    width = 3 * D_MODEL
    return pl.pallas_call(
        functools.partial(_prep_kernel, nt, tm, t_pad),
        grid=(nt,),
        in_specs=[
            pl.BlockSpec((tm, width), lambda i: (i, 0)),
            pl.BlockSpec((hb, width), lambda i: (jnp.maximum(i * per - 1, 0), 0)),
            pl.BlockSpec((hb, width), lambda i: (jnp.minimum(i * per + per, n_halo - 1), 0)),
            pl.BlockSpec((8, width), lambda i: (0, 0)),
        ],
        out_specs=pl.BlockSpec((tm, width), lambda i: (i, 0)),
        out_shape=jax.ShapeDtypeStruct((rows, width), BF16),
        scratch_shapes=[pltpu.VMEM((tm + 2 * hb, width), F32)],
        compiler_params=pltpu.CompilerParams(
            dimension_semantics=("parallel",), vmem_limit_bytes=VMEM_LIMIT),
        name="qkvprep",
    )(p, p, p, cw)


def _dn_kernel(nc, qf_ref, qb_ref, baf_ref, bab_ref, gp_ref, of_ref, ob_ref, s_ref):
    n = pl.program_id(1)

    @pl.when(n == 0)
    def _():
        s_ref[...] = jnp.zeros_like(s_ref)

    rid = lax.broadcasted_iota(jnp.int32, (CHUNK, 1), 0)
    ri = lax.broadcasted_iota(jnp.int32, (CHUNK, CHUNK), 0)
    ci = lax.broadcasted_iota(jnp.int32, (CHUNK, CHUNK), 1)
    lane = lax.broadcasted_iota(jnp.int32, (1, 128), 1)
    a_lanes = jnp.logical_and(lane >= HEADS, lane < 2 * HEADS)
    dirs = ((0, qf_ref, baf_ref, of_ref, n), (1, qb_ref, bab_ref, ob_ref, nc - 1 - n))

    gate = {}
    for d, _, ba_ref, _, c in dirs:
        ba = ba_ref[...]
        valid = jnp.logical_or(c > 0, rid >= PAD_FRONT).astype(F32)
        neg_a = jnp.where(a_lanes, -jnp.exp(gp_ref[d, 0:1, :]), 0.0)
        g = neg_a * jax.nn.softplus(ba + gp_ref[d, 1:2, :]) * valid
        tri = (ri >= ci) if d == 0 else (ri <= ci)
        strict = (ri > ci) if d == 0 else (ri < ci)
        g_hi = g.astype(BF16)
        r1 = g - g_hi.astype(F32)
        g_mid = r1.astype(BF16)
        g_lo = (r1 - g_mid.astype(F32)).astype(BF16)
        tri_b = tri.astype(BF16)
        gc = _dot(jnp.concatenate([tri_b, tri_b, tri_b], axis=1), jnp.concatenate([g_hi, g_mid, g_lo], axis=0))
        gl = jnp.sum(g, axis=0, keepdims=True)
        gate[d] = dict(beta=jax.nn.sigmoid(ba), gc=gc, gc_t=gc.T, eg=jnp.exp(gc), egl=jnp.exp(gl - gc),
                       eg_last=jnp.exp(gl), tri=tri, strict=strict)

    insts = [(d, h) for d in (0, 1) for h in range(HEADS)]

    def head(ref, part, h):
        return ref[:, part * D_MODEL + h * HEAD_DIM:part * D_MODEL + (h + 1) * HEAD_DIM]

    qb = {(d, h): head(dirs[d][1], 0, h) for d, h in insts}
    kb = {(d, h): head(dirs[d][1], 1, h) for d, h in insts}
    vb = {(d, h): head(dirs[d][1], 2, h) for d, h in insts}

    def col(d, name, lane_idx):
        return gate[d][name][:, lane_idx:lane_idx + 1]

    decay = {}
    for d, h in insts:
        ga = HEADS + h
        g_row = gate[d]["gc_t"][ga:ga + 1, :]
        decay[d, h] = jnp.exp(jnp.where(gate[d]["tri"], col(d, "gc", ga) - g_row, -jnp.inf))

    kkqk = {i: lax.dot_general(jnp.concatenate([kb[i], qb[i]], axis=0), kb[i], (((1,), (1,)), ((), ())),
                               preferred_element_type=F32) for i in insts}

    y = {}
    pw = {}
    for d, h in insts:
        nm = jnp.where(gate[d]["strict"], -(col(d, "beta", h) * kkqk[d, h][:CHUNK] * decay[d, h]), 0.0)
        y[d, h] = nm
        pw[d, h] = nm.astype(BF16)
    pw = {i: _dot(pw[i], pw[i]) for i in insts}
    for j in range(5):
        if j < 4:
            x2 = {i: _dot(jnp.concatenate([pw[i].astype(BF16), y[i].astype(BF16)], axis=0), pw[i].astype(BF16))
                  for i in insts}
            y = {i: y[i] + pw[i] + x2[i][CHUNK:] for i in insts}
            pw = {i: x2[i][:CHUNK] for i in insts}
        else:
            yp = {i: _dot(y[i].astype(BF16), pw[i].astype(BF16)) for i in insts}
            y = {i: y[i] + pw[i] + yp[i] for i in insts}

    sol = {}
    for d, h in insts:
        b_col = col(d, "beta", h)
        rhs = jnp.concatenate([vb[d, h].astype(F32) * b_col,
                               kb[d, h].astype(F32) * (b_col * col(d, "eg", HEADS + h))], axis=1)
        sol[d, h] = rhs + _dot(y[d, h].astype(BF16), rhs.astype(BF16))

    s_old = {(d, h): s_ref[d, h] for d, h in insts}
    sb = {i: s_old[i].astype(BF16) for i in insts}
    ws = {i: _dot(sol[i][:, HEAD_DIM:].astype(BF16), sb[i]) for i in insts}
    qs = {(d, h): _dot((qb[d, h].astype(F32) * col(d, "eg", HEADS + h)).astype(BF16), sb[d, h]) for d, h in insts}
    v_new = {i: (sol[i][:, :HEAD_DIM] - ws[i]).astype(BF16) for i in insts}
    o = {i: qs[i] + _dot((kkqk[i][CHUNK:] * decay[i]).astype(BF16), v_new[i]) for i in insts}
    for d, h in insts:
        ga = HEADS + h
        kd = (kb[d, h].astype(F32) * col(d, "egl", ga)).astype(BF16)
        s_ref[d, h] = s_old[d, h] * gate[d]["eg_last"][:, ga:ga + 1] + lax.dot_general(
            kd, v_new[d, h], (((0,), (0,)), ((), ())), preferred_element_type=F32)
        dirs[d][3][:, h * HEAD_DIM:(h + 1) * HEAD_DIM] = o[d, h].astype(BF16)


def _deltanet(qkv, ba, gp, nb, nc):
    rows = qkv.shape[0]
    width = 3 * D_MODEL

    def fwd(b, n):
        return b * nc + n

    def bwd(b, n):
        return b * nc + nc - 1 - n

    out = jax.ShapeDtypeStruct((rows, D_MODEL), BF16)
    return pl.pallas_call(
        functools.partial(_dn_kernel, nc),
        grid=(nb, nc),
        in_specs=[
            pl.BlockSpec((CHUNK, width), lambda b, n: (fwd(b, n), 0)),
            pl.BlockSpec((CHUNK, width), lambda b, n: (bwd(b, n), 0)),
            pl.BlockSpec((CHUNK, 128), lambda b, n: (fwd(b, n), 0)),
            pl.BlockSpec((CHUNK, 128), lambda b, n: (bwd(b, n), 1)),
            pl.BlockSpec((2, 8, 128), lambda b, n: (0, 0, 0)),
        ],
        out_specs=[
            pl.BlockSpec((CHUNK, D_MODEL), lambda b, n: (fwd(b, n), 0)),
            pl.BlockSpec((CHUNK, D_MODEL), lambda b, n: (bwd(b, n), 0)),
        ],
        out_shape=[out, out],
        scratch_shapes=[pltpu.VMEM((2, HEADS, HEAD_DIM, HEAD_DIM), F32)],
        compiler_params=pltpu.CompilerParams(
            dimension_semantics=("parallel", "arbitrary"), vmem_limit_bytes=VMEM_LIMIT),
        name="deltanet",
    )(qkv, qkv, ba, ba, gp)


def _merge_kernel(nt, tm, of_ref, ob_ref, z_ref, scb_ref, scc_ref, scx_ref, sccp_ref, scxp_ref, sccn_ref, scxn_ref,
                  ga_ref, gb_ref, h_ref, nw_ref, cw_ref, wdn_ref, wsc_ref, wout_ref, out_ref, xb_ref):
    is_last = pl.program_id(0) == nt - 1

    o = of_ref[...].astype(F32) + ob_ref[...].astype(F32)
    z = z_ref[...].astype(F32)
    parts = []
    for h in range(HEADS):
        sl = slice(h * HEAD_DIM, (h + 1) * HEAD_DIM)
        parts.append((_rms(o[:, sl], nw_ref[...]) * _silu(z[:, sl])).astype(BF16))
    y_a = _dot(jnp.concatenate(parts, axis=1), wdn_ref[...])

    cx = scc_ref[...].astype(F32) * scx_ref[...].astype(F32)
    hb = BF16_SUBLANES
    prev = sccp_ref[...].astype(F32) * scxp_ref[...].astype(F32)
    nxt = sccn_ref[...].astype(F32) * scxn_ref[...].astype(F32)
    xb_ref[8:8 + tm, :] = cx
    xb_ref[7:8, :] = prev[hb - 1:hb, :]
    xb_ref[8 + tm:9 + tm, :] = jnp.where(is_last, 0.0, nxt[0:1, :])
    conv = cw_ref[0:1, :] * xb_ref[7:7 + tm, :] + cw_ref[1:2, :] * cx + cw_ref[2:3, :] * xb_ref[9:9 + tm, :]
    y_b = _dot((scb_ref[...].astype(F32) * conv).astype(BF16), wsc_ref[...])

    merged = (jax.nn.sigmoid(ga_ref[...].astype(F32)) * y_a
              + jax.nn.sigmoid(gb_ref[...].astype(F32)) * y_b)
    out_ref[...] = h_ref[...] + _dot(merged.astype(BF16), wout_ref[...])


def _merge(o_f, o_b, p, hres, nw, cw, wdn, wsc, wout, tm):
    rows = hres.shape[0]
    nt = rows // tm
    hb = BF16_SUBLANES
    per = tm // hb
    n_halo = rows // hb

    def col(c):
        return pl.BlockSpec((tm, D_MODEL), lambda i: (i, c))

    def prev(c):
        return pl.BlockSpec((hb, D_MODEL), lambda i: (jnp.maximum(i * per - 1, 0), c))

    def nxt(c):
        return pl.BlockSpec((hb, D_MODEL), lambda i: (jnp.minimum(i * per + per, n_halo - 1), c))

    def full(shape):
        return pl.BlockSpec(shape, lambda i: (0,) * len(shape))

    return pl.pallas_call(
        functools.partial(_merge_kernel, nt, tm),
        grid=(nt,),
        in_specs=[
            col(0), col(0),
            col(3), col(4), col(5), col(6), prev(5), prev(6), nxt(5), nxt(6), col(7), col(8),
            pl.BlockSpec((tm, D_MODEL), lambda i: (i, 0)),
            full((1, HEAD_DIM)), full((8, D_MODEL)),
            full((D_MODEL, D_MODEL)), full((D_MODEL, D_MODEL)), full((D_MODEL, D_MODEL)),
        ],
        out_specs=pl.BlockSpec((tm, D_MODEL), lambda i: (i, 0)),
        out_shape=jax.ShapeDtypeStruct((rows, D_MODEL), F32),
        scratch_shapes=[pltpu.VMEM((tm + 16, D_MODEL), F32)],
        input_output_aliases={12: 0},
        compiler_params=pltpu.CompilerParams(
            dimension_semantics=("parallel",), vmem_limit_bytes=VMEM_LIMIT),
        name="merge",
    )(o_f, o_b, p, p, p, p, p, p, p, p, p, p, hres, nw, cw, wdn, wsc, wout)


def _mlp_kernel(final, x_ref, nw_ref, wg_ref, wu_ref, wd_ref, fw_ref, out_ref):
    x = x_ref[...]
    h2 = _rms(x, nw_ref[...]).astype(BF16)
    acc = x
    half = D_FF // 2
    for f in range(2):
        sl = slice(f * half, (f + 1) * half)
        a = _silu(_dot(h2, wg_ref[:, sl])) * _dot(h2, wu_ref[:, sl])
        acc = acc + _dot(a.astype(BF16), wd_ref[sl, :])
    if final:
        acc = _rms(acc, fw_ref[...])
    out_ref[...] = acc


def _mlp(x, nw, wg, wu, wd, fw, final, tm):
    rows = x.shape[0]

    def full(shape):
        return pl.BlockSpec(shape, lambda i: (0,) * len(shape))

    return pl.pallas_call(
        functools.partial(_mlp_kernel, final),
        grid=(rows // tm,),
        in_specs=[
            pl.BlockSpec((tm, D_MODEL), lambda i: (i, 0)),
            full((1, D_MODEL)), full((D_MODEL, D_FF)), full((D_MODEL, D_FF)), full((D_FF, D_MODEL)),
            full((1, D_MODEL)),
        ],
        out_specs=pl.BlockSpec((tm, D_MODEL), lambda i: (i, 0)),
        out_shape=jax.ShapeDtypeStruct((rows, D_MODEL), F32),
        input_output_aliases={0: 0},
        compiler_params=pltpu.CompilerParams(
            dimension_semantics=("parallel",), vmem_limit_bytes=VMEM_LIMIT),
        name="mlp",
    )(x, nw, wg, wu, wd, fw)


def _row_tile(rows, want):
    t = want - want % BF16_SUBLANES
    while rows % t:
        t -= BF16_SUBLANES
    return t


def _pad_rows(a, n):
    return jnp.pad(a, ((0, n - a.shape[0]), (0, 0)))


def kernel(x, meta_tokens, norm1_w, w_in, dn_conv_w, A_log, dt_bias, dn_norm_w, sc_conv_w, w_branch_dn,
           w_branch_sc, w_out, norm2_w, w_gate_up, w_down, final_norm_w):
    nb, seq, _ = x.shape
    depth = w_in.shape[0]
    t_pad = PAD_FRONT + N_META + seq
    nc = t_pad // CHUNK
    rows = nb * t_pad

    meta = jnp.broadcast_to(meta_tokens[None].astype(x.dtype), (nb, N_META, D_MODEL))
    hres = jnp.concatenate([jnp.zeros((nb, PAD_FRONT, D_MODEL), x.dtype), meta, x], axis=1)
    hres = hres.reshape(rows, D_MODEL)

    q_cols = 4 * D_MODEL
    ba_cols = 4 * HEADS
    tm_in = _row_tile(rows, 768)
    tm_prep = _row_tile(rows, 256)
    tm_merge = _row_tile(rows, 256)
    tm_mlp = _row_tile(rows, 512)

    for l in range(depth):
        w_main = jnp.concatenate([w_in[l, :, :q_cols], w_in[l, :, q_cols + ba_cols:]], axis=1).astype(BF16)
        w_b = w_in[l, :, q_cols:q_cols + 2 * HEADS]
        w_a = w_in[l, :, q_cols + 2 * HEADS:q_cols + ba_cols]
        zpad = jnp.zeros((D_MODEL, 128 - 2 * HEADS), w_in.dtype)
        w_ba = jnp.concatenate([w_b[:, :HEADS], w_a[:, :HEADS], zpad,
                                w_b[:, HEADS:], w_a[:, HEADS:], zpad], axis=1).astype(BF16)
        p, ba = _inproj(hres, norm1_w[l][None], w_main, w_ba, tm_in, 1536)

        qkv = _prep(p, _pad_rows(dn_conv_w[l], 8), tm_prep, t_pad)
        lane_pad = ((0, 0), (HEADS, 128 - 2 * HEADS))
        gp = jnp.stack([jnp.pad(A_log[l], lane_pad), jnp.pad(dt_bias[l], lane_pad)], axis=1)
        gp = jnp.pad(gp, ((0, 0), (0, 6), (0, 0))).astype(F32)
        o_f, o_b = _deltanet(qkv, ba, gp, nb, nc)

        hres = _merge(o_f, o_b, p, hres, dn_norm_w[l][None], _pad_rows(sc_conv_w[l], 8),
                      w_branch_dn[l].astype(BF16), w_branch_sc[l].astype(BF16), w_out[l].astype(BF16), tm_merge)

        hres = _mlp(hres, norm2_w[l][None], w_gate_up[l, :, :D_FF].astype(BF16),
                    w_gate_up[l, :, D_FF:].astype(BF16), w_down[l].astype(BF16), final_norm_w[None],
                    l == depth - 1, tm_mlp)

    return hres.reshape(nb, t_pad, D_MODEL)[:, PAD_FRONT + N_META:]
```

```python
import functools

import jax
import jax.numpy as jnp
from jax import lax
from jax.experimental import pallas as pl
from jax.experimental.pallas import tpu as pltpu

F32 = jnp.float32
BF16 = jnp.bfloat16

D_MODEL = 1024
N_META = 16
CHUNK = 64
PAD_FRONT = (-N_META) % CHUNK
HEADS = 8
HEAD_DIM = 128
DN_CONV = 5
SC_CONV = 3
D_FF = 2816
RMS_EPS = 1e-6
L2_EPS = 1e-6

DN_BATCH = 2
BF16_SUBLANES = 16
VMEM_LIMIT = 56 * 1024 * 1024


def _silu(x):
    return x * jax.nn.sigmoid(x)


def _rms(x, w):
    return x * lax.rsqrt(jnp.mean(x * x, axis=-1, keepdims=True) + RMS_EPS) * w


def _dot(a, b):
    return jnp.dot(a, b, preferred_element_type=F32)


def _inproj_kernel(x_ref, nw_ref, w_ref, wba_ref, p_ref, ba_ref, h_ref):
    @pl.when(pl.program_id(1) == 0)
    def _():
        h = _rms(x_ref[...], nw_ref[...]).astype(BF16)
        h_ref[...] = h
        ba_ref[...] = _dot(h, wba_ref[...])

    p_ref[...] = _dot(h_ref[...], w_ref[...]).astype(BF16)


def _inproj(x, nw, w, wba, tm, tn):
    rows, n = x.shape[0], w.shape[1]
    return pl.pallas_call(
        _inproj_kernel,
        grid=(rows // tm, n // tn),
        in_specs=[
            pl.BlockSpec((tm, D_MODEL), lambda i, j: (i, 0)),
            pl.BlockSpec((1, D_MODEL), lambda i, j: (0, 0)),
            pl.BlockSpec((D_MODEL, tn), lambda i, j: (0, j)),
            pl.BlockSpec((D_MODEL, 256), lambda i, j: (0, 0)),
        ],
        out_specs=[
            pl.BlockSpec((tm, tn), lambda i, j: (i, j)),
            pl.BlockSpec((tm, 256), lambda i, j: (i, 0)),
        ],
        out_shape=[
            jax.ShapeDtypeStruct((rows, n), BF16),
            jax.ShapeDtypeStruct((rows, 256), F32),
        ],
        scratch_shapes=[pltpu.VMEM((tm, D_MODEL), BF16)],
        compiler_params=pltpu.CompilerParams(
            dimension_semantics=("parallel", "arbitrary"), vmem_limit_bytes=VMEM_LIMIT),
        name="inproj",
    )(x, nw, w, wba)


def _prep_kernel(nt, tm, t_pad, pm_ref, pp_ref, pn_ref, cw_ref, o_ref, xb_ref):
    i = pl.program_id(0)
    hb = BF16_SUBLANES
    width = 3 * D_MODEL
    xb_ref[0:hb, :] = pp_ref[...]
    xb_ref[hb:hb + tm, :] = pm_ref[...]
    xb_ref[hb + tm:, :] = jnp.where(i == nt - 1, jnp.zeros((), BF16), pn_ref[...])

    win = CHUNK + 2 * hb
    r = DN_CONV // 2
    ri = lax.broadcasted_iota(jnp.int32, (CHUNK, DN_CONV * win), 0)
    ci = lax.broadcasted_iota(jnp.int32, (CHUNK, DN_CONV * win), 1)
    shift = jnp.zeros((CHUNK, DN_CONV * win), jnp.bool_)
    for s in range(DN_CONV):
        shift = jnp.logical_or(shift, ci == ri + (s * win + hb - r + s))

    for blk in range(tm // CHUNK):
        x = xb_ref[blk * CHUNK:blk * CHUNK + win, :].reshape(win // hb, hb, width)
        taps = [(x * cw_ref[s][None]).reshape(win, width) for s in range(DN_CONV)]
        pos = lax.rem(i * tm + blk * CHUNK, t_pad) + ri
        pos = jnp.where(pos >= t_pad, pos - t_pad, pos)
        shift_v = jnp.where(jnp.logical_and(shift, pos >= PAD_FRONT), 1.0, 0.0).astype(BF16)
        qkv = _silu(_dot(shift_v, jnp.concatenate(taps, axis=0)))

        rows = slice(blk * CHUNK, (blk + 1) * CHUNK)
        for h in range(HEADS):
            for part in range(3):
                sl = slice(part * D_MODEL + h * HEAD_DIM, part * D_MODEL + (h + 1) * HEAD_DIM)
                t = qkv[:, sl]
                if part == 0:
                    t = t * (lax.rsqrt(jnp.sum(t * t, axis=-1, keepdims=True) + L2_EPS) * (HEAD_DIM ** -0.5))
                elif part == 1:
                    t = t * lax.rsqrt(jnp.sum(t * t, axis=-1, keepdims=True) + L2_EPS)
                o_ref[rows, sl] = t.astype(BF16)


def _prep(p, cw, tm, t_pad):
    rows = p.shape[0]
    nt = rows // tm
    hb = BF16_SUBLANES
    per = tm // hb
    n_halo = rows // hb
    width = 3 * D_MODEL
    return pl.pallas_call(
        functools.partial(_prep_kernel, nt, tm, t_pad),
        grid=(nt,),
        in_specs=[
            pl.BlockSpec((tm, width), lambda i: (i, 0)),
            pl.BlockSpec((hb, width), lambda i: (jnp.maximum(i * per - 1, 0), 0)),
            pl.BlockSpec((hb, width), lambda i: (jnp.minimum(i * per + per, n_halo - 1), 0)),
            pl.BlockSpec((DN_CONV, hb, width), lambda i: (0, 0, 0)),
        ],
        out_specs=pl.BlockSpec((tm, width), lambda i: (i, 0)),
        out_shape=jax.ShapeDtypeStruct((rows, width), BF16),
        scratch_shapes=[pltpu.VMEM((tm + 2 * hb, width), BF16)],
        compiler_params=pltpu.CompilerParams(
            dimension_semantics=("parallel",), vmem_limit_bytes=VMEM_LIMIT),
        name="qkvprep",
    )(p, p, p, cw)


def _dn_kernel(nc, qf_ref, qb_ref, baf_ref, bab_ref, gp_ref, of_ref, ob_ref, s_ref):
    n = pl.program_id(1)

    @pl.when(n == 0)
    def _():
        s_ref[...] = jnp.zeros_like(s_ref)

    rid = lax.broadcasted_iota(jnp.int32, (CHUNK, 1), 0)
    ri = lax.broadcasted_iota(jnp.int32, (CHUNK, CHUNK), 0)
    ci = lax.broadcasted_iota(jnp.int32, (CHUNK, CHUNK), 1)
    lane = lax.broadcasted_iota(jnp.int32, (1, 128), 1)
    a_lanes = jnp.logical_and(lane >= HEADS, lane < 2 * HEADS)
    dirs = []
    for bi in range(DN_BATCH):
        dirs.append((2 * bi, qf_ref, baf_ref, of_ref, n, bi, 0))
        dirs.append((2 * bi + 1, qb_ref, bab_ref, ob_ref, nc - 1 - n, bi, 1))
    n_streams = len(dirs)

    gate = {}
    for d, _, ba_ref, _, c, bi, dirn in dirs:
        ba = ba_ref[bi]
        valid = jnp.logical_or(c > 0, rid >= PAD_FRONT).astype(F32)
        neg_a = jnp.where(a_lanes, -jnp.exp(gp_ref[dirn, 0:1, :]), 0.0)
        g = neg_a * jax.nn.softplus(ba + gp_ref[dirn, 1:2, :]) * valid
        tri = (ri >= ci) if dirn == 0 else (ri <= ci)
        g_hi = g.astype(BF16)
        r1 = g - g_hi.astype(F32)
        g_mid = r1.astype(BF16)
        g_lo = (r1 - g_mid.astype(F32)).astype(BF16)
        tri_b = tri.astype(BF16)
        gc = _dot(jnp.concatenate([tri_b, tri_b, tri_b], axis=1), jnp.concatenate([g_hi, g_mid, g_lo], axis=0))
        gl = jnp.sum(g, axis=0, keepdims=True)
        beta = jax.nn.sigmoid(ba)
        gate[d] = dict(beta=beta, beta_t=beta.T, gc=gc, gc_t=gc.T, eg=jnp.exp(gc), egl=jnp.exp(gl - gc),
                       eg_last=jnp.exp(gl))

    heads = [(d, h) for d in range(n_streams) for h in range(HEADS)]
    pairs = [(d, t) for d in range(n_streams) for t in range(HEADS // 2)]
    pw_lanes = 2 * CHUNK

    def head(d, part, h, n_heads=1):
        lo = part * D_MODEL + h * HEAD_DIM
        return dirs[d][1][dirs[d][5], :, lo:lo + n_heads * HEAD_DIM]

    def col(d, name, lane_idx):
        return gate[d][name][:, lane_idx:lane_idx + 1]

    def pair_row(d, name, base, t):
        g_t = gate[d][name]
        return jnp.concatenate([g_t[base + 2 * t:base + 2 * t + 1, :], g_t[base + 2 * t + 1:base + 2 * t + 2, :]], axis=1)

    lane_p = lax.broadcasted_iota(jnp.int32, (CHUNK, pw_lanes), 1)
    row_p = lax.broadcasted_iota(jnp.int32, (CHUNK, pw_lanes), 0)
    first_half = lane_p < CHUNK
    col_p = jnp.where(first_half, lane_p, lane_p - CHUNK)
    tri_p = {0: row_p >= col_p, 1: row_p <= col_p}
    strict_p = {0: row_p > col_p, 1: row_p < col_p}
    r2 = lax.broadcasted_iota(jnp.int32, (2 * CHUNK, pw_lanes), 0)
    c2 = lax.broadcasted_iota(jnp.int32, (2 * CHUNK, pw_lanes), 1)
    bd_p = (r2 >= CHUNK) == (c2 >= CHUNK)
    r3 = lax.broadcasted_iota(jnp.int32, (2 * CHUNK, 2 * HEAD_DIM), 0)
    c3 = lax.broadcasted_iota(jnp.int32, (2 * CHUNK, 2 * HEAD_DIM), 1)
    bd_k = (r3 >= CHUNK) == (c3 >= HEAD_DIM)

    def block_diag(pair):
        return jnp.where(bd_p, jnp.concatenate([pair, pair], axis=0), jnp.zeros((), pair.dtype))

    kkqk = {}
    for d, t in pairs:
        k2 = head(d, 1, 2 * t, 2)
        q2 = head(d, 0, 2 * t, 2)
        k_bd = jnp.where(bd_k, jnp.concatenate([k2, k2], axis=0), jnp.zeros((), BF16))
        kkqk[d, t] = lax.dot_general(jnp.concatenate([k2, q2], axis=0), k_bd, (((1,), (1,)), ((), ())),
                                     preferred_element_type=F32)

    decay = {}
    for d, t in pairs:
        ga = HEADS + 2 * t
        g_col = jnp.where(first_half, col(d, "gc", ga), col(d, "gc", ga + 1))
        decay[d, t] = jnp.exp(jnp.where(tri_p[dirs[d][6]], g_col - pair_row(d, "gc_t", HEADS, t), -jnp.inf))
    beta_row = {(d, t): pair_row(d, "beta_t", 0, t) for d, t in pairs}

    eye_p = (row_p == col_p).astype(F32)
    inv = {}
    pw = {}
    for i in pairs:
        nm = jnp.where(strict_p[dirs[i[0]][6]], -(kkqk[i][:CHUNK] * decay[i] * beta_row[i]), 0.0)
        inv[i] = nm + eye_p
        nb16 = nm.astype(BF16)
        pw[i] = _dot(nb16, block_diag(nb16))
    for j in range(5):
        if j < 4:
            x2 = {}
            for i in pairs:
                pb = pw[i].astype(BF16)
                x2[i] = _dot(jnp.concatenate([pb, inv[i].astype(BF16)], axis=0), block_diag(pb))
            inv = {i: inv[i] + x2[i][CHUNK:] for i in pairs}
            pw = {i: x2[i][:CHUNK] for i in pairs}
        else:
            inv = {i: inv[i] + _dot(inv[i].astype(BF16), block_diag(pw[i].astype(BF16))) for i in pairs}

    sol = {}
    for d, t in pairs:
        rhs = []
        for h in (2 * t, 2 * t + 1):
            k_eg = (head(d, 1, h).astype(F32) * col(d, "eg", HEADS + h)).astype(BF16)
            rhs.append(jnp.concatenate([head(d, 2, h), k_eg], axis=1))
        both = _dot(block_diag(inv[d, t].astype(BF16)), jnp.concatenate(rhs, axis=0))
        sol[d, 2 * t] = both[:CHUNK]
        sol[d, 2 * t + 1] = both[CHUNK:]

    s_old = {(d, h): s_ref[dirs[d][5], dirs[d][6], h] for d, h in heads}
    ws_qs = {}
    for d, h in heads:
        qd = (head(d, 0, h).astype(F32) * col(d, "eg", HEADS + h)).astype(BF16)
        ws_qs[d, h] = _dot(jnp.concatenate([sol[d, h][:, HEAD_DIM:].astype(BF16), qd], axis=0), s_old[d, h].astype(BF16))
    v_hat = {i: (sol[i][:, :HEAD_DIM] - ws_qs[i][:CHUNK]).astype(BF16) for i in heads}
    intra = {}
    for d, t in pairs:
        qkd = (kkqk[d, t][CHUNK:] * decay[d, t] * beta_row[d, t]).astype(BF16)
        both = _dot(block_diag(qkd), jnp.concatenate([v_hat[d, 2 * t], v_hat[d, 2 * t + 1]], axis=0))
        intra[d, 2 * t] = both[:CHUNK]
        intra[d, 2 * t + 1] = both[CHUNK:]
    for d, h in heads:
        ga = HEADS + h
        kd = (head(d, 1, h).astype(F32) * (col(d, "egl", ga) * col(d, "beta", h))).astype(BF16)
        s_ref[dirs[d][5], dirs[d][6], h] = s_old[d, h] * gate[d]["eg_last"][:, ga:ga + 1] + lax.dot_general(
            kd, v_hat[d, h], (((0,), (0,)), ((), ())), preferred_element_type=F32)
        dirs[d][3][dirs[d][5], :, h * HEAD_DIM:(h + 1) * HEAD_DIM] = (ws_qs[d, h][CHUNK:] + intra[d, h]).astype(BF16)


def _deltanet(qkv, ba, gp, nb, nc):
    t_pad = qkv.shape[1]
    width = 3 * D_MODEL
    out = jax.ShapeDtypeStruct((nb, t_pad, D_MODEL), BF16)
    return pl.pallas_call(
        functools.partial(_dn_kernel, nc),
        grid=(nb // DN_BATCH, nc),
        in_specs=[
            pl.BlockSpec((DN_BATCH, CHUNK, width), lambda b, n: (b, n, 0)),
            pl.BlockSpec((DN_BATCH, CHUNK, width), lambda b, n: (b, nc - 1 - n, 0)),
            pl.BlockSpec((DN_BATCH, CHUNK, 128), lambda b, n: (b, n, 0)),
            pl.BlockSpec((DN_BATCH, CHUNK, 128), lambda b, n: (b, nc - 1 - n, 1)),
            pl.BlockSpec((2, 8, 128), lambda b, n: (0, 0, 0)),
        ],
        out_specs=[
            pl.BlockSpec((DN_BATCH, CHUNK, D_MODEL), lambda b, n: (b, n, 0)),
            pl.BlockSpec((DN_BATCH, CHUNK, D_MODEL), lambda b, n: (b, nc - 1 - n, 0)),
        ],
        out_shape=[out, out],
        scratch_shapes=[pltpu.VMEM((DN_BATCH, 2, HEADS, HEAD_DIM, HEAD_DIM), F32)],
        compiler_params=pltpu.CompilerParams(
            dimension_semantics=("parallel", "arbitrary"), vmem_limit_bytes=VMEM_LIMIT),
        name="deltanet",
    )(qkv, qkv, ba, ba, gp)


def _merge_kernel(nt, tm, of_ref, ob_ref, z_ref, scb_ref, scc_ref, scx_ref, sccp_ref, scxp_ref, sccn_ref, scxn_ref,
                  ga_ref, gb_ref, h_ref, nw_ref, cw_ref, wdn_ref, wsc_ref, wout_ref, out_ref, xb_ref):
    is_last = pl.program_id(0) == nt - 1

    o = of_ref[...].astype(F32) + ob_ref[...].astype(F32)
    z = z_ref[...].astype(F32)
    parts = []
    for h in range(HEADS):
        sl = slice(h * HEAD_DIM, (h + 1) * HEAD_DIM)
        parts.append((_rms(o[:, sl], nw_ref[...]) * _silu(z[:, sl])).astype(BF16))
    y_a = _dot(jnp.concatenate(parts, axis=1), wdn_ref[...])

    hb = BF16_SUBLANES
    xb_ref[0:hb, :] = sccp_ref[...] * scxp_ref[...]
    xb_ref[hb:hb + tm, :] = scc_ref[...] * scx_ref[...]
    xb_ref[hb + tm:, :] = jnp.where(is_last, jnp.zeros((), BF16), sccn_ref[...] * scxn_ref[...])
    win = CHUNK + 2 * hb
    r = SC_CONV // 2
    ri = lax.broadcasted_iota(jnp.int32, (CHUNK, SC_CONV * win), 0)
    ci = lax.broadcasted_iota(jnp.int32, (CHUNK, SC_CONV * win), 1)
    shift = jnp.zeros((CHUNK, SC_CONV * win), jnp.bool_)
    for s in range(SC_CONV):
        shift = jnp.logical_or(shift, ci == ri + (s * win + hb - r + s))
    shift = jnp.where(shift, 1.0, 0.0).astype(BF16)
    gated = []
    for blk in range(tm // CHUNK):
        x = xb_ref[blk * CHUNK:blk * CHUNK + win, :].reshape(win // hb, hb, D_MODEL)
        taps = [(x * cw_ref[s][None]).reshape(win, D_MODEL) for s in range(SC_CONV)]
        conv = _dot(shift, jnp.concatenate(taps, axis=0))
        gated.append(scb_ref[blk * CHUNK:(blk + 1) * CHUNK, :] * conv.astype(BF16))
    y_b = _dot(jnp.concatenate(gated, axis=0), wsc_ref[...])

    merged = (jax.nn.sigmoid(ga_ref[...].astype(F32)) * y_a
              + jax.nn.sigmoid(gb_ref[...].astype(F32)) * y_b)
    out_ref[...] = h_ref[...] + _dot(merged.astype(BF16), wout_ref[...])


def _merge(o_f, o_b, p, hres, nw, cw, wdn, wsc, wout, tm):
    rows = hres.shape[0]
    nt = rows // tm
    hb = BF16_SUBLANES
    per = tm // hb
    n_halo = rows // hb

    def col(c):
        return pl.BlockSpec((tm, D_MODEL), lambda i: (i, c))

    def prev(c):
        return pl.BlockSpec((hb, D_MODEL), lambda i: (jnp.maximum(i * per - 1, 0), c))

    def nxt(c):
        return pl.BlockSpec((hb, D_MODEL), lambda i: (jnp.minimum(i * per + per, n_halo - 1), c))

    def full(shape):
        return pl.BlockSpec(shape, lambda i: (0,) * len(shape))

    return pl.pallas_call(
        functools.partial(_merge_kernel, nt, tm),
        grid=(nt,),
        in_specs=[
            col(0), col(0),
            col(3), col(4), col(5), col(6), prev(5), prev(6), nxt(5), nxt(6), col(7), col(8),
            pl.BlockSpec((tm, D_MODEL), lambda i: (i, 0)),
            full((1, HEAD_DIM)), full((SC_CONV, hb, D_MODEL)),
            full((D_MODEL, D_MODEL)), full((D_MODEL, D_MODEL)), full((D_MODEL, D_MODEL)),
        ],
        out_specs=pl.BlockSpec((tm, D_MODEL), lambda i: (i, 0)),
        out_shape=jax.ShapeDtypeStruct((rows, D_MODEL), F32),
        scratch_shapes=[pltpu.VMEM((tm + 2 * hb, D_MODEL), BF16)],
        input_output_aliases={12: 0},
        compiler_params=pltpu.CompilerParams(
            dimension_semantics=("parallel",), vmem_limit_bytes=VMEM_LIMIT),
        name="merge",
    )(o_f, o_b, p, p, p, p, p, p, p, p, p, p, hres, nw, cw, wdn, wsc, wout)


def _mlp_kernel(final, x_ref, nw_ref, wg_ref, wu_ref, wd_ref, fw_ref, out_ref):
    x = x_ref[...].reshape(x_ref.shape[-2:])
    h2 = _rms(x, nw_ref[...]).astype(BF16)
    acc = x
    half = D_FF // 2
    for f in range(2):
        sl = slice(f * half, (f + 1) * half)
        a = _silu(_dot(h2, wg_ref[:, sl])) * _dot(h2, wu_ref[:, sl])
        acc = acc + _dot(a.astype(BF16), wd_ref[sl, :])
    if final:
        acc = _rms(acc, fw_ref[...])
    out_ref[...] = acc


def _mlp(x, nw, wg, wu, wd, fw, tm):
    rows = x.shape[0]

    def full(shape):
        return pl.BlockSpec(shape, lambda i: (0,) * len(shape))

    return pl.pallas_call(
        functools.partial(_mlp_kernel, False),
        grid=(rows // tm,),
        in_specs=[
            pl.BlockSpec((tm, D_MODEL), lambda i: (i, 0)),
            full((1, D_MODEL)), full((D_MODEL, D_FF)), full((D_MODEL, D_FF)), full((D_FF, D_MODEL)),
            full((1, D_MODEL)),
        ],
        out_specs=pl.BlockSpec((tm, D_MODEL), lambda i: (i, 0)),
        out_shape=jax.ShapeDtypeStruct((rows, D_MODEL), F32),
        input_output_aliases={0: 0},
        compiler_params=pltpu.CompilerParams(
            dimension_semantics=("parallel",), vmem_limit_bytes=VMEM_LIMIT),
        name="mlp",
    )(x, nw, wg, wu, wd, fw)


def _mlp_final(x, nw, wg, wu, wd, fw, nb, seq, tm):
    t_pad = x.shape[0] // nb
    front = t_pad - seq

    def full(shape):
        return pl.BlockSpec(shape, lambda b, j: (0,) * len(shape))

    return pl.pallas_call(
        functools.partial(_mlp_kernel, True),
        grid=(nb, seq // tm),
        in_specs=[
            pl.BlockSpec((pl.Element(tm), pl.Element(D_MODEL)),
                         lambda b, j: (pl.multiple_of(b * t_pad + front + j * tm, 8), 0)),
            full((1, D_MODEL)), full((D_MODEL, D_FF)), full((D_MODEL, D_FF)), full((D_FF, D_MODEL)),
            full((1, D_MODEL)),
        ],
        out_specs=pl.BlockSpec((None, tm, D_MODEL), lambda b, j: (b, j, 0)),
        out_shape=jax.ShapeDtypeStruct((nb, seq, D_MODEL), F32),
        compiler_params=pltpu.CompilerParams(
            dimension_semantics=("parallel", "parallel"), vmem_limit_bytes=VMEM_LIMIT),
        name="mlp_final",
    )(x, nw, wg, wu, wd, fw)


def _row_tile(rows, want, step=BF16_SUBLANES):
    t = want - want % step
    while rows % t:
        t -= step
    return t


def kernel(x, meta_tokens, norm1_w, w_in, dn_conv_w, A_log, dt_bias, dn_norm_w, sc_conv_w, w_branch_dn,
           w_branch_sc, w_out, norm2_w, w_gate_up, w_down, final_norm_w):
    nb, seq, _ = x.shape
    depth = w_in.shape[0]
    t_pad = PAD_FRONT + N_META + seq
    nc = t_pad // CHUNK
    rows = nb * t_pad

    meta = jnp.broadcast_to(meta_tokens[None].astype(x.dtype), (nb, N_META, D_MODEL))
    hres = jnp.concatenate([jnp.zeros((nb, PAD_FRONT, D_MODEL), x.dtype), meta, x], axis=1)
    hres = hres.reshape(rows, D_MODEL)

    q_cols = 4 * D_MODEL
    ba_cols = 4 * HEADS
    tm_in = _row_tile(rows, 768)
    tm_prep = _row_tile(rows, 256, CHUNK)
    tm_merge = _row_tile(rows, 512, CHUNK)
    tm_mlp = _row_tile(rows, 512)

    for l in range(depth):
        w_main = jnp.concatenate([w_in[l, :, :q_cols], w_in[l, :, q_cols + ba_cols:]], axis=1).astype(BF16)
        w_b = w_in[l, :, q_cols:q_cols + 2 * HEADS]
        w_a = w_in[l, :, q_cols + 2 * HEADS:q_cols + ba_cols]
        zpad = jnp.zeros((D_MODEL, 128 - 2 * HEADS), w_in.dtype)
        w_ba = jnp.concatenate([w_b[:, :HEADS], w_a[:, :HEADS], zpad,
                                w_b[:, HEADS:], w_a[:, HEADS:], zpad], axis=1).astype(BF16)
        p, ba = _inproj(hres, norm1_w[l][None], w_main, w_ba, tm_in, 1536)

        cw = jnp.broadcast_to(dn_conv_w[l].astype(BF16)[:, None, :], (DN_CONV, BF16_SUBLANES, 3 * D_MODEL))
        qkv = _prep(p, cw, tm_prep, t_pad)
        lane_pad = ((0, 0), (HEADS, 128 - 2 * HEADS))
        gp = jnp.stack([jnp.pad(A_log[l], lane_pad), jnp.pad(dt_bias[l], lane_pad)], axis=1)
        gp = jnp.pad(gp, ((0, 0), (0, 6), (0, 0))).astype(F32)
        o_f, o_b = _deltanet(qkv.reshape(nb, t_pad, 3 * D_MODEL), ba.reshape(nb, t_pad, 256), gp, nb, nc)
        o_f = o_f.reshape(rows, D_MODEL)
        o_b = o_b.reshape(rows, D_MODEL)

        sc_cw = jnp.broadcast_to(sc_conv_w[l].astype(BF16)[:, None, :], (SC_CONV, BF16_SUBLANES, D_MODEL))
        hres = _merge(o_f, o_b, p, hres, dn_norm_w[l][None], sc_cw,
                      w_branch_dn[l].astype(BF16), w_branch_sc[l].astype(BF16), w_out[l].astype(BF16), tm_merge)

        mlp_w = (norm2_w[l][None], w_gate_up[l, :, :D_FF].astype(BF16), w_gate_up[l, :, D_FF:].astype(BF16),
                 w_down[l].astype(BF16), final_norm_w[None])
        if l < depth - 1:
            hres = _mlp(hres, *mlp_w, tm_mlp)
        else:
            out = _mlp_final(hres, *mlp_w, nb, seq, _row_tile(seq, 512))
    return out
```

```python
import functools

import jax
import jax.numpy as jnp
from jax import lax
from jax.experimental import pallas as pl
from jax.experimental.pallas import tpu as pltpu

F32 = jnp.float32
BF16 = jnp.bfloat16

D_MODEL = 1024
N_META = 16
CHUNK = 64
PAD_FRONT = (-N_META) % CHUNK
HEADS = 8
HEAD_DIM = 128
DN_CONV = 5
SC_CONV = 3
D_FF = 2816
RMS_EPS = 1e-6
L2_EPS = 1e-6

DN_BATCH = 4
FF_CHUNK = 256
BF16_SUBLANES = 16
VMEM_LIMIT = 56 * 1024 * 1024


def _silu(x):
    return x * jax.nn.sigmoid(x)


def _rms(x, w):
    return x * lax.rsqrt(jnp.mean(x * x, axis=-1, keepdims=True) + RMS_EPS) * w


def _dot(a, b):
    return jnp.dot(a, b, preferred_element_type=F32)


def _inproj_kernel(x_ref, nw_ref, w_ref, wba_ref, p_ref, ba_ref, h_ref):
    @pl.when(pl.program_id(1) == 0)
    def _():
        h = _rms(x_ref[...], nw_ref[...]).astype(BF16)
        h_ref[...] = h
        ba_ref[...] = _dot(h, wba_ref[...])

    p_ref[...] = _dot(h_ref[...], w_ref[...]).astype(BF16)


def _inproj(x, nw, w, wba, tm, tn):
    rows, n = x.shape[0], w.shape[1]
    return pl.pallas_call(
        _inproj_kernel,
        grid=(rows // tm, n // tn),
        in_specs=[
            pl.BlockSpec((tm, D_MODEL), lambda i, j: (i, 0)),
            pl.BlockSpec((1, D_MODEL), lambda i, j: (0, 0)),
            pl.BlockSpec((D_MODEL, tn), lambda i, j: (0, j)),
            pl.BlockSpec((D_MODEL, 256), lambda i, j: (0, 0)),
        ],
        out_specs=[
            pl.BlockSpec((tm, tn), lambda i, j: (i, j)),
            pl.BlockSpec((tm, 256), lambda i, j: (i, 0)),
        ],
        out_shape=[
            jax.ShapeDtypeStruct((rows, n), BF16),
            jax.ShapeDtypeStruct((rows, 256), F32),
        ],
        scratch_shapes=[pltpu.VMEM((tm, D_MODEL), BF16)],
        compiler_params=pltpu.CompilerParams(
            dimension_semantics=("parallel", "arbitrary"), vmem_limit_bytes=VMEM_LIMIT),
        name="inproj",
    )(x, nw, w, wba)


def _tap_groups(n_taps):
    hb = BF16_SUBLANES
    r = n_taps // 2
    return [((hb - r + s) // hb, (hb - r + s + CHUNK - 1) // hb - (hb - r + s) // hb + 1) for s in range(n_taps)]


def _shift_matrix(n_taps):
    hb = BF16_SUBLANES
    r = n_taps // 2
    groups = _tap_groups(n_taps)
    k = sum(n for _, n in groups) * hb
    ri = lax.broadcasted_iota(jnp.int32, (CHUNK, k), 0)
    ci = lax.broadcasted_iota(jnp.int32, (CHUNK, k), 1)
    hit = jnp.zeros((CHUNK, k), jnp.bool_)
    base = 0
    for s, (first, n) in enumerate(groups):
        hit = jnp.logical_or(hit, ci == ri + (base + hb - r + s - first * hb))
        base += n * hb
    return hit, ri


def _weighted_taps(x, cw_ref, n_taps):
    taps = []
    for s, (first, n) in enumerate(_tap_groups(n_taps)):
        taps.append((x[first:first + n] * cw_ref[s][None]).reshape(n * BF16_SUBLANES, x.shape[-1]))
    return jnp.concatenate(taps, axis=0)


def _prep_kernel(nt, tm, t_pad, pm_ref, pp_ref, pn_ref, cw_ref, o_ref, xb_ref):
    i = pl.program_id(0)
    hb = BF16_SUBLANES
    width = 3 * D_MODEL
    xb_ref[0:hb, :] = pp_ref[...]
    xb_ref[hb:hb + tm, :] = pm_ref[...]
    xb_ref[hb + tm:, :] = jnp.where(i == nt - 1, jnp.zeros((), BF16), pn_ref[...])

    win = CHUNK + 2 * hb
    shift, ri = _shift_matrix(DN_CONV)
    for blk in range(tm // CHUNK):
        x = xb_ref[blk * CHUNK:blk * CHUNK + win, :].reshape(win // hb, hb, width)
        taps = _weighted_taps(x, cw_ref, DN_CONV)
        pos = lax.rem(i * tm + blk * CHUNK, t_pad) + ri
        pos = jnp.where(pos >= t_pad, pos - t_pad, pos)
        shift_v = jnp.where(jnp.logical_and(shift, pos >= PAD_FRONT), 1.0, 0.0).astype(BF16)
        qkv = _silu(_dot(shift_v, taps))

        rows = slice(blk * CHUNK, (blk + 1) * CHUNK)
        for h in range(HEADS):
            for part in range(3):
                sl = slice(part * D_MODEL + h * HEAD_DIM, part * D_MODEL + (h + 1) * HEAD_DIM)
                t = qkv[:, sl]
                if part == 0:
                    t = t * (lax.rsqrt(jnp.sum(t * t, axis=-1, keepdims=True) + L2_EPS) * (HEAD_DIM ** -0.5))
                elif part == 1:
                    t = t * lax.rsqrt(jnp.sum(t * t, axis=-1, keepdims=True) + L2_EPS)
                o_ref[rows, sl] = t.astype(BF16)


def _prep(p, cw, tm, t_pad):
    rows = p.shape[0]
    nt = rows // tm
    hb = BF16_SUBLANES
    per = tm // hb
    n_halo = rows // hb
    width = 3 * D_MODEL
    return pl.pallas_call(
        functools.partial(_prep_kernel, nt, tm, t_pad),
        grid=(nt,),
        in_specs=[
            pl.BlockSpec((tm, width), lambda i: (i, 0)),
            pl.BlockSpec((hb, width), lambda i: (jnp.maximum(i * per - 1, 0), 0)),
            pl.BlockSpec((hb, width), lambda i: (jnp.minimum(i * per + per, n_halo - 1), 0)),
            pl.BlockSpec((DN_CONV, hb, width), lambda i: (0, 0, 0)),
        ],
        out_specs=pl.BlockSpec((tm, width), lambda i: (i, 0)),
        out_shape=jax.ShapeDtypeStruct((rows, width), BF16),
        scratch_shapes=[pltpu.VMEM((tm + 2 * hb, width), BF16)],
        compiler_params=pltpu.CompilerParams(
            dimension_semantics=("parallel",), vmem_limit_bytes=VMEM_LIMIT),
        name="qkvprep",
    )(p, p, p, cw)


def _dn_kernel(nc, qf_ref, qb_ref, baf_ref, bab_ref, gp_ref, of_ref, ob_ref, s_ref):
    n = pl.program_id(1)

    @pl.when(n == 0)
    def _():
        s_ref[...] = jnp.zeros_like(s_ref)

    rid = lax.broadcasted_iota(jnp.int32, (CHUNK, 1), 0)
    ri = lax.broadcasted_iota(jnp.int32, (CHUNK, CHUNK), 0)
    ci = lax.broadcasted_iota(jnp.int32, (CHUNK, CHUNK), 1)
    lane = lax.broadcasted_iota(jnp.int32, (1, 128), 1)
    a_lanes = jnp.logical_and(lane >= HEADS, lane < 2 * HEADS)
    dirs = []
    for bi in range(DN_BATCH):
        dirs.append((2 * bi, qf_ref, baf_ref, of_ref, n, bi, 0))
        dirs.append((2 * bi + 1, qb_ref, bab_ref, ob_ref, nc - 1 - n, bi, 1))
    n_streams = len(dirs)

    gate = {}
    for d, _, ba_ref, _, c, bi, dirn in dirs:
        ba = ba_ref[bi]
        valid = jnp.logical_or(c > 0, rid >= PAD_FRONT).astype(F32)
        neg_a = jnp.where(a_lanes, -jnp.exp(gp_ref[dirn, 0:1, :]), 0.0)
        g = neg_a * jax.nn.softplus(ba + gp_ref[dirn, 1:2, :]) * valid
        tri = (ri >= ci) if dirn == 0 else (ri <= ci)
        g_hi = g.astype(BF16)
        r1 = g - g_hi.astype(F32)
        g_mid = r1.astype(BF16)
        g_lo = (r1 - g_mid.astype(F32)).astype(BF16)
        tri_b = tri.astype(BF16)
        gc = _dot(jnp.concatenate([tri_b, tri_b, tri_b], axis=1), jnp.concatenate([g_hi, g_mid, g_lo], axis=0))
        gl = jnp.sum(g, axis=0, keepdims=True)
        beta = jax.nn.sigmoid(ba)
        gate[d] = dict(beta=beta, beta_t=beta.T, gc=gc, gc_t=gc.T, eg=jnp.exp(gc), egl=jnp.exp(gl - gc),
                       eg_last=jnp.exp(gl))

    heads = [(d, h) for d in range(n_streams) for h in range(HEADS)]
    pairs = [(d, t) for d in range(n_streams) for t in range(HEADS // 2)]
    pw_lanes = 2 * CHUNK

    def head(d, part, h, n_heads=1):
        lo = part * D_MODEL + h * HEAD_DIM
        return dirs[d][1][dirs[d][5], :, lo:lo + n_heads * HEAD_DIM]

    def col(d, name, lane_idx):
        return gate[d][name][:, lane_idx:lane_idx + 1]

    def pair_row(d, name, base, t):
        g_t = gate[d][name]
        return jnp.concatenate([g_t[base + 2 * t:base + 2 * t + 1, :], g_t[base + 2 * t + 1:base + 2 * t + 2, :]], axis=1)

    lane_p = lax.broadcasted_iota(jnp.int32, (CHUNK, pw_lanes), 1)
    row_p = lax.broadcasted_iota(jnp.int32, (CHUNK, pw_lanes), 0)
    first_half = lane_p < CHUNK
    col_p = jnp.where(first_half, lane_p, lane_p - CHUNK)
    tri_p = {0: row_p >= col_p, 1: row_p <= col_p}
    strict_p = {0: row_p > col_p, 1: row_p < col_p}
    r2 = lax.broadcasted_iota(jnp.int32, (2 * CHUNK, pw_lanes), 0)
    c2 = lax.broadcasted_iota(jnp.int32, (2 * CHUNK, pw_lanes), 1)
    bd_p = (r2 >= CHUNK) == (c2 >= CHUNK)
    r3 = lax.broadcasted_iota(jnp.int32, (2 * CHUNK, 2 * HEAD_DIM), 0)
    c3 = lax.broadcasted_iota(jnp.int32, (2 * CHUNK, 2 * HEAD_DIM), 1)
    bd_k = (r3 >= CHUNK) == (c3 >= HEAD_DIM)

    def block_diag(pair):
        return jnp.where(bd_p, jnp.concatenate([pair, pair], axis=0), jnp.zeros((), pair.dtype))

    kkqk = {}
    for d, t in pairs:
        k2 = head(d, 1, 2 * t, 2)
        q2 = head(d, 0, 2 * t, 2)
        k_bd = jnp.where(bd_k, jnp.concatenate([k2, k2], axis=0), jnp.zeros((), BF16))
        kkqk[d, t] = lax.dot_general(jnp.concatenate([k2, q2], axis=0), k_bd, (((1,), (1,)), ((), ())),
                                     preferred_element_type=F32)

    decay = {}
    for d, t in pairs:
        ga = HEADS + 2 * t
        g_col = jnp.where(first_half, col(d, "gc", ga), col(d, "gc", ga + 1))
        decay[d, t] = jnp.exp(jnp.where(tri_p[dirs[d][6]], g_col - pair_row(d, "gc_t", HEADS, t), -jnp.inf))
    beta_row = {(d, t): pair_row(d, "beta_t", 0, t) for d, t in pairs}

    eye_p = (row_p == col_p).astype(F32)
    inv = {}
    pw = {}
    for i in pairs:
        nm = jnp.where(strict_p[dirs[i[0]][6]], -(kkqk[i][:CHUNK] * decay[i] * beta_row[i]), 0.0)
        inv[i] = nm + eye_p
        nb16 = nm.astype(BF16)
        pw[i] = _dot(nb16, block_diag(nb16))
    for j in range(5):
        if j < 4:
            x2 = {}
            for i in pairs:
                pb = pw[i].astype(BF16)
                x2[i] = _dot(jnp.concatenate([pb, inv[i].astype(BF16)], axis=0), block_diag(pb))
            inv = {i: inv[i] + x2[i][CHUNK:] for i in pairs}
            pw = {i: x2[i][:CHUNK] for i in pairs}
        else:
            inv = {i: inv[i] + _dot(inv[i].astype(BF16), block_diag(pw[i].astype(BF16))) for i in pairs}

    sol = {}
    for d, t in pairs:
        rhs = []
        for h in (2 * t, 2 * t + 1):
            k_eg = (head(d, 1, h).astype(F32) * col(d, "eg", HEADS + h)).astype(BF16)
            rhs.append(jnp.concatenate([head(d, 2, h), k_eg], axis=1))
        both = _dot(block_diag(inv[d, t].astype(BF16)), jnp.concatenate(rhs, axis=0))
        sol[d, 2 * t] = both[:CHUNK]
        sol[d, 2 * t + 1] = both[CHUNK:]

    s_old = {(d, h): s_ref[dirs[d][5], dirs[d][6], h] for d, h in heads}
    ws_qs = {}
    for d, h in heads:
        qd = (head(d, 0, h).astype(F32) * col(d, "eg", HEADS + h)).astype(BF16)
        ws_qs[d, h] = _dot(jnp.concatenate([sol[d, h][:, HEAD_DIM:].astype(BF16), qd], axis=0), s_old[d, h].astype(BF16))
    v_hat = {i: (sol[i][:, :HEAD_DIM] - ws_qs[i][:CHUNK]).astype(BF16) for i in heads}
    intra = {}
    for d, t in pairs:
        qkd = (kkqk[d, t][CHUNK:] * decay[d, t] * beta_row[d, t]).astype(BF16)
        both = _dot(block_diag(qkd), jnp.concatenate([v_hat[d, 2 * t], v_hat[d, 2 * t + 1]], axis=0))
        intra[d, 2 * t] = both[:CHUNK]
        intra[d, 2 * t + 1] = both[CHUNK:]
    for d, h in heads:
        ga = HEADS + h
        kd = (head(d, 1, h).astype(F32) * (col(d, "egl", ga) * col(d, "beta", h))).astype(BF16)
        s_ref[dirs[d][5], dirs[d][6], h] = s_old[d, h] * gate[d]["eg_last"][:, ga:ga + 1] + lax.dot_general(
            kd, v_hat[d, h], (((0,), (0,)), ((), ())), preferred_element_type=F32)
        dirs[d][3][dirs[d][5], :, h * HEAD_DIM:(h + 1) * HEAD_DIM] = (ws_qs[d, h][CHUNK:] + intra[d, h]).astype(BF16)


def _deltanet(qkv, ba, gp, nb, nc):
    t_pad = qkv.shape[1]
    width = 3 * D_MODEL
    out = jax.ShapeDtypeStruct((nb, t_pad, D_MODEL), BF16)
    return pl.pallas_call(
        functools.partial(_dn_kernel, nc),
        grid=(nb // DN_BATCH, nc),
        in_specs=[
            pl.BlockSpec((DN_BATCH, CHUNK, width), lambda b, n: (b, n, 0)),
            pl.BlockSpec((DN_BATCH, CHUNK, width), lambda b, n: (b, nc - 1 - n, 0)),
            pl.BlockSpec((DN_BATCH, CHUNK, 128), lambda b, n: (b, n, 0)),
            pl.BlockSpec((DN_BATCH, CHUNK, 128), lambda b, n: (b, nc - 1 - n, 1)),
            pl.BlockSpec((2, 8, 128), lambda b, n: (0, 0, 0)),
        ],
        out_specs=[
            pl.BlockSpec((DN_BATCH, CHUNK, D_MODEL), lambda b, n: (b, n, 0)),
            pl.BlockSpec((DN_BATCH, CHUNK, D_MODEL), lambda b, n: (b, nc - 1 - n, 0)),
        ],
        out_shape=[out, out],
        scratch_shapes=[pltpu.VMEM((DN_BATCH, 2, HEADS, HEAD_DIM, HEAD_DIM), F32)],
        compiler_params=pltpu.CompilerParams(
            dimension_semantics=("parallel", "arbitrary"), vmem_limit_bytes=VMEM_LIMIT),
        name="deltanet",
    )(qkv, qkv, ba, ba, gp)


def _merge_kernel(nt, tm, of_ref, ob_ref, z_ref, scb_ref, scc_ref, scx_ref, sccp_ref, scxp_ref, sccn_ref, scxn_ref,
                  ga_ref, gb_ref, h_ref, nw_ref, cw_ref, wdn_ref, wsc_ref, wout_ref, out_ref, xb_ref):
    is_last = pl.program_id(0) == nt - 1

    hb = BF16_SUBLANES
    xb_ref[0:hb, :] = sccp_ref[...] * scxp_ref[...]
    xb_ref[hb:hb + tm, :] = scc_ref[...] * scx_ref[...]
    xb_ref[hb + tm:, :] = jnp.where(is_last, jnp.zeros((), BF16), sccn_ref[...] * scxn_ref[...])
    win = CHUNK + 2 * hb
    shift = jnp.where(_shift_matrix(SC_CONV)[0], 1.0, 0.0).astype(BF16)

    o = of_ref[...].astype(F32) + ob_ref[...].astype(F32)
    z = z_ref[...].astype(F32)
    parts = []
    for h in range(HEADS):
        sl = slice(h * HEAD_DIM, (h + 1) * HEAD_DIM)
        parts.append((_rms(o[:, sl], nw_ref[...]) * _silu(z[:, sl])).astype(BF16))
    y_a = _dot(jnp.concatenate(parts, axis=1), wdn_ref[...])

    gated = []
    for blk in range(tm // CHUNK):
        x = xb_ref[blk * CHUNK:blk * CHUNK + win, :].reshape(win // hb, hb, D_MODEL)
        conv = _dot(shift, _weighted_taps(x, cw_ref, SC_CONV))
        gated.append(scb_ref[blk * CHUNK:(blk + 1) * CHUNK, :] * conv.astype(BF16))
    y_b = _dot(jnp.concatenate(gated, axis=0), wsc_ref[...])

    merged = (jax.nn.sigmoid(ga_ref[...].astype(F32)) * y_a
              + jax.nn.sigmoid(gb_ref[...].astype(F32)) * y_b)
    out_ref[...] = h_ref[...] + _dot(merged.astype(BF16), wout_ref[...])


def _merge(o_f, o_b, p, hres, nw, cw, wdn, wsc, wout, tm):
    rows = hres.shape[0]
    nt = rows // tm
    hb = BF16_SUBLANES
    per = tm // hb
    n_halo = rows // hb

    def col(c):
        return pl.BlockSpec((tm, D_MODEL), lambda i: (i, c))

    def prev(c):
        return pl.BlockSpec((hb, D_MODEL), lambda i: (jnp.maximum(i * per - 1, 0), c))

    def nxt(c):
        return pl.BlockSpec((hb, D_MODEL), lambda i: (jnp.minimum(i * per + per, n_halo - 1), c))

    def full(shape):
        return pl.BlockSpec(shape, lambda i: (0,) * len(shape))

    return pl.pallas_call(
        functools.partial(_merge_kernel, nt, tm),
        grid=(nt,),
        in_specs=[
            col(0), col(0),
            col(3), col(4), col(5), col(6), prev(5), prev(6), nxt(5), nxt(6), col(7), col(8),
            pl.BlockSpec((tm, D_MODEL), lambda i: (i, 0)),
            full((1, HEAD_DIM)), full((SC_CONV, hb, D_MODEL)),
            full((D_MODEL, D_MODEL)), full((D_MODEL, D_MODEL)), full((D_MODEL, D_MODEL)),
        ],
        out_specs=pl.BlockSpec((tm, D_MODEL), lambda i: (i, 0)),
        out_shape=jax.ShapeDtypeStruct((rows, D_MODEL), F32),
        scratch_shapes=[pltpu.VMEM((tm + 2 * hb, D_MODEL), BF16)],
        input_output_aliases={12: 0},
        compiler_params=pltpu.CompilerParams(
            dimension_semantics=("parallel",), vmem_limit_bytes=VMEM_LIMIT),
        name="merge",
    )(o_f, o_b, p, p, p, p, p, p, p, p, p, p, hres, nw, cw, wdn, wsc, wout)


def _mlp_kernel(final, x_ref, nw_ref, wg_ref, wu_ref, wd_ref, fw_ref, out_ref):
    x = x_ref[...].reshape(x_ref.shape[-2:])
    h2 = _rms(x, nw_ref[...]).astype(BF16)
    acc = x
    for lo in range(0, D_FF, FF_CHUNK):
        sl = slice(lo, lo + FF_CHUNK)
        a = _silu(_dot(h2, wg_ref[:, sl])) * _dot(h2, wu_ref[:, sl])
        acc = acc + _dot(a.astype(BF16), wd_ref[sl, :])
    if final:
        acc = _rms(acc, fw_ref[...])
    out_ref[...] = acc


def _mlp(x, nw, wg, wu, wd, fw, tm):
    rows = x.shape[0]

    def full(shape):
        return pl.BlockSpec(shape, lambda i: (0,) * len(shape))

    return pl.pallas_call(
        functools.partial(_mlp_kernel, False),
        grid=(rows // tm,),
        in_specs=[
            pl.BlockSpec((tm, D_MODEL), lambda i: (i, 0)),
            full((1, D_MODEL)), full((D_MODEL, D_FF)), full((D_MODEL, D_FF)), full((D_FF, D_MODEL)),
            full((1, D_MODEL)),
        ],
        out_specs=pl.BlockSpec((tm, D_MODEL), lambda i: (i, 0)),
        out_shape=jax.ShapeDtypeStruct((rows, D_MODEL), F32),
        input_output_aliases={0: 0},
        compiler_params=pltpu.CompilerParams(
            dimension_semantics=("parallel",), vmem_limit_bytes=VMEM_LIMIT),
        name="mlp",
    )(x, nw, wg, wu, wd, fw)


def _mlp_final(x, nw, wg, wu, wd, fw, nb, seq, tm):
    t_pad = x.shape[0] // nb
    front = t_pad - seq

    def full(shape):
        return pl.BlockSpec(shape, lambda b, j: (0,) * len(shape))

    return pl.pallas_call(
        functools.partial(_mlp_kernel, True),
        grid=(nb, seq // tm),
        in_specs=[
            pl.BlockSpec((pl.Element(tm), pl.Element(D_MODEL)),
                         lambda b, j: (pl.multiple_of(b * t_pad + front + j * tm, 8), 0)),
            full((1, D_MODEL)), full((D_MODEL, D_FF)), full((D_MODEL, D_FF)), full((D_FF, D_MODEL)),
            full((1, D_MODEL)),
        ],
        out_specs=pl.BlockSpec((None, tm, D_MODEL), lambda b, j: (b, j, 0)),
        out_shape=jax.ShapeDtypeStruct((nb, seq, D_MODEL), F32),
        compiler_params=pltpu.CompilerParams(
            dimension_semantics=("parallel", "parallel"), vmem_limit_bytes=VMEM_LIMIT),
        name="mlp_final",
    )(x, nw, wg, wu, wd, fw)


def _row_tile(rows, want, step=BF16_SUBLANES):
    t = want - want % step
    while rows % t:
        t -= step
    return t


def kernel(x, meta_tokens, norm1_w, w_in, dn_conv_w, A_log, dt_bias, dn_norm_w, sc_conv_w, w_branch_dn,
           w_branch_sc, w_out, norm2_w, w_gate_up, w_down, final_norm_w):
    nb, seq, _ = x.shape
    depth = w_in.shape[0]
    t_pad = PAD_FRONT + N_META + seq
    nc = t_pad // CHUNK
    rows = nb * t_pad

    meta = jnp.broadcast_to(meta_tokens[None].astype(x.dtype), (nb, N_META, D_MODEL))
    hres = jnp.concatenate([jnp.zeros((nb, PAD_FRONT, D_MODEL), x.dtype), meta, x], axis=1)
    hres = hres.reshape(rows, D_MODEL)

    q_cols = 4 * D_MODEL
    ba_cols = 4 * HEADS
    tm_in = _row_tile(rows, 1536)
    tm_prep = _row_tile(rows, 512, CHUNK)
    tm_merge = _row_tile(rows, 512, CHUNK)
    tm_mlp = _row_tile(rows, 512)

    for l in range(depth):
        w_main = jnp.concatenate([w_in[l, :, :q_cols], w_in[l, :, q_cols + ba_cols:]], axis=1).astype(BF16)
        w_b = w_in[l, :, q_cols:q_cols + 2 * HEADS]
        w_a = w_in[l, :, q_cols + 2 * HEADS:q_cols + ba_cols]
        zpad = jnp.zeros((D_MODEL, 128 - 2 * HEADS), w_in.dtype)
        w_ba = jnp.concatenate([w_b[:, :HEADS], w_a[:, :HEADS], zpad,
                                w_b[:, HEADS:], w_a[:, HEADS:], zpad], axis=1).astype(BF16)
        p, ba = _inproj(hres, norm1_w[l][None], w_main, w_ba, tm_in, 1536)

        cw = jnp.broadcast_to(dn_conv_w[l].astype(BF16)[:, None, :], (DN_CONV, BF16_SUBLANES, 3 * D_MODEL))
        qkv = _prep(p, cw, tm_prep, t_pad)
        lane_pad = ((0, 0), (HEADS, 128 - 2 * HEADS))
        gp = jnp.stack([jnp.pad(A_log[l], lane_pad), jnp.pad(dt_bias[l], lane_pad)], axis=1)
        gp = jnp.pad(gp, ((0, 0), (0, 6), (0, 0))).astype(F32)
        o_f, o_b = _deltanet(qkv.reshape(nb, t_pad, 3 * D_MODEL), ba.reshape(nb, t_pad, 256), gp, nb, nc)
        o_f = o_f.reshape(rows, D_MODEL)
        o_b = o_b.reshape(rows, D_MODEL)

        sc_cw = jnp.broadcast_to(sc_conv_w[l].astype(BF16)[:, None, :], (SC_CONV, BF16_SUBLANES, D_MODEL))
        hres = _merge(o_f, o_b, p, hres, dn_norm_w[l][None], sc_cw,
                      w_branch_dn[l].astype(BF16), w_branch_sc[l].astype(BF16), w_out[l].astype(BF16), tm_merge)

        mlp_w = (norm2_w[l][None], w_gate_up[l, :, :D_FF].astype(BF16), w_gate_up[l, :, D_FF:].astype(BF16),
                 w_down[l].astype(BF16), final_norm_w[None])
        if l < depth - 1:
            hres = _mlp(hres, *mlp_w, tm_mlp)
        else:
            out = _mlp_final(hres, *mlp_w, nb, seq, _row_tile(seq, 512))
    return out
```

```python
import functools

import jax
import jax.numpy as jnp
from jax import lax
from jax.experimental import pallas as pl
from jax.experimental.pallas import tpu as pltpu

F32 = jnp.float32
BF16 = jnp.bfloat16

D_MODEL = 1024
N_META = 16
CHUNK = 64
PAD_FRONT = (-N_META) % CHUNK
HEADS = 8
HEAD_DIM = 128
DN_CONV = 5
SC_CONV = 3
D_FF = 2816
RMS_EPS = 1e-6
L2_EPS = 1e-6

DN_BATCH = 8
INV_BLOCK = 16
FF_CHUNK = 256
BF16_SUBLANES = 16
VMEM_LIMIT = 56 * 1024 * 1024


def _silu(x):
    return x * jax.nn.sigmoid(x)


def _rms(x, w):
    return x * lax.rsqrt(jnp.mean(x * x, axis=-1, keepdims=True) + RMS_EPS) * w


def _dot(a, b):
    return jnp.dot(a, b, preferred_element_type=F32)


def _inproj_kernel(x_ref, nw_ref, w_ref, wba_ref, p_ref, ba_ref, h_ref):
    @pl.when(pl.program_id(1) == 0)
    def _():
        h = _rms(x_ref[...], nw_ref[...]).astype(BF16)
        h_ref[...] = h
        ba_ref[...] = _dot(h, wba_ref[...])

    p_ref[...] = _dot(h_ref[...], w_ref[...]).astype(BF16)


def _inproj(x, nw, w, wba, tm, tn):
    rows, n = x.shape[0], w.shape[1]
    return pl.pallas_call(
        _inproj_kernel,
        grid=(rows // tm, n // tn),
        in_specs=[
            pl.BlockSpec((tm, D_MODEL), lambda i, j: (i, 0)),
            pl.BlockSpec((1, D_MODEL), lambda i, j: (0, 0)),
            pl.BlockSpec((D_MODEL, tn), lambda i, j: (0, j)),
            pl.BlockSpec((D_MODEL, 256), lambda i, j: (0, 0)),
        ],
        out_specs=[
            pl.BlockSpec((tm, tn), lambda i, j: (i, j)),
            pl.BlockSpec((tm, 256), lambda i, j: (i, 0)),
        ],
        out_shape=[
            jax.ShapeDtypeStruct((rows, n), BF16),
            jax.ShapeDtypeStruct((rows, 256), F32),
        ],
        scratch_shapes=[pltpu.VMEM((tm, D_MODEL), BF16)],
        compiler_params=pltpu.CompilerParams(
            dimension_semantics=("parallel", "arbitrary"), vmem_limit_bytes=VMEM_LIMIT),
        name="inproj",
    )(x, nw, w, wba)


def _tap_groups(n_taps):
    hb = BF16_SUBLANES
    r = n_taps // 2
    return [((hb - r + s) // hb, (hb - r + s + CHUNK - 1) // hb - (hb - r + s) // hb + 1) for s in range(n_taps)]


def _shift_matrix(n_taps):
    hb = BF16_SUBLANES
    r = n_taps // 2
    groups = _tap_groups(n_taps)
    k = sum(n for _, n in groups) * hb
    ri = lax.broadcasted_iota(jnp.int32, (CHUNK, k), 0)
    ci = lax.broadcasted_iota(jnp.int32, (CHUNK, k), 1)
    hit = jnp.zeros((CHUNK, k), jnp.bool_)
    base = 0
    for s, (first, n) in enumerate(groups):
        hit = jnp.logical_or(hit, ci == ri + (base + hb - r + s - first * hb))
        base += n * hb
    return hit, ri


def _weighted_taps(x, cw_ref, n_taps):
    taps = []
    for s, (first, n) in enumerate(_tap_groups(n_taps)):
        taps.append((x[first:first + n] * cw_ref[s][None]).reshape(n * BF16_SUBLANES, x.shape[-1]))
    return jnp.concatenate(taps, axis=0)


def _prep_kernel(nt, tm, t_pad, pm_ref, pp_ref, pn_ref, cw_ref, o_ref, xb_ref):
    i = pl.program_id(0)
    hb = BF16_SUBLANES
    width = 3 * D_MODEL
    xb_ref[0:hb, :] = pp_ref[...]
    xb_ref[hb:hb + tm, :] = pm_ref[...]
    xb_ref[hb + tm:, :] = jnp.where(i == nt - 1, jnp.zeros((), BF16), pn_ref[...])

    win = CHUNK + 2 * hb
    shift, ri = _shift_matrix(DN_CONV)
    for blk in range(tm // CHUNK):
        x = xb_ref[blk * CHUNK:blk * CHUNK + win, :].reshape(win // hb, hb, width)
        taps = _weighted_taps(x, cw_ref, DN_CONV)
        pos = lax.rem(i * tm + blk * CHUNK, t_pad) + ri
        pos = jnp.where(pos >= t_pad, pos - t_pad, pos)
        shift_v = jnp.where(jnp.logical_and(shift, pos >= PAD_FRONT), 1.0, 0.0).astype(BF16)
        qkv = _silu(_dot(shift_v, taps))

        rows = slice(blk * CHUNK, (blk + 1) * CHUNK)
        for h in range(HEADS):
            for part in range(3):
                sl = slice(part * D_MODEL + h * HEAD_DIM, part * D_MODEL + (h + 1) * HEAD_DIM)
                t = qkv[:, sl]
                if part == 0:
                    t = t * (lax.rsqrt(jnp.sum(t * t, axis=-1, keepdims=True) + L2_EPS) * (HEAD_DIM ** -0.5))
                elif part == 1:
                    t = t * lax.rsqrt(jnp.sum(t * t, axis=-1, keepdims=True) + L2_EPS)
                o_ref[rows, sl] = t.astype(BF16)


def _prep(p, cw, tm, t_pad):
    rows = p.shape[0]
    nt = rows // tm
    hb = BF16_SUBLANES
    per = tm // hb
    n_halo = rows // hb
    width = 3 * D_MODEL
    return pl.pallas_call(
        functools.partial(_prep_kernel, nt, tm, t_pad),
        grid=(nt,),
        in_specs=[
            pl.BlockSpec((tm, width), lambda i: (i, 0)),
            pl.BlockSpec((hb, width), lambda i: (jnp.maximum(i * per - 1, 0), 0)),
            pl.BlockSpec((hb, width), lambda i: (jnp.minimum(i * per + per, n_halo - 1), 0)),
            pl.BlockSpec((DN_CONV, hb, width), lambda i: (0, 0, 0)),
        ],
        out_specs=pl.BlockSpec((tm, width), lambda i: (i, 0)),
        out_shape=jax.ShapeDtypeStruct((rows, width), BF16),
        scratch_shapes=[pltpu.VMEM((tm + 2 * hb, width), BF16)],
        compiler_params=pltpu.CompilerParams(
            dimension_semantics=("parallel",), vmem_limit_bytes=VMEM_LIMIT),
        name="qkvprep",
    )(p, p, p, cw)


def _dn_kernel(nc, qf_ref, qb_ref, baf_ref, bab_ref, gp_ref, of_ref, ob_ref, s_ref):
    n = pl.program_id(1)

    @pl.when(n == 0)
    def _():
        s_ref[...] = jnp.zeros_like(s_ref)

    rid = lax.broadcasted_iota(jnp.int32, (CHUNK, 1), 0)
    ri = lax.broadcasted_iota(jnp.int32, (CHUNK, CHUNK), 0)
    ci = lax.broadcasted_iota(jnp.int32, (CHUNK, CHUNK), 1)
    lane = lax.broadcasted_iota(jnp.int32, (1, 128), 1)
    a_lanes = jnp.logical_and(lane >= HEADS, lane < 2 * HEADS)
    dirs = []
    for bi in range(DN_BATCH):
        dirs.append((2 * bi, qf_ref, baf_ref, of_ref, n, bi, 0))
        dirs.append((2 * bi + 1, qb_ref, bab_ref, ob_ref, nc - 1 - n, bi, 1))
    n_streams = len(dirs)

    gate = {}
    for d, _, ba_ref, _, c, bi, dirn in dirs:
        ba = ba_ref[bi]
        valid = jnp.logical_or(c > 0, rid >= PAD_FRONT).astype(F32)
        neg_a = jnp.where(a_lanes, -jnp.exp(gp_ref[dirn, 0:1, :]), 0.0)
        g = neg_a * jax.nn.softplus(ba + gp_ref[dirn, 1:2, :]) * valid
        tri = (ri >= ci) if dirn == 0 else (ri <= ci)
        g_hi = g.astype(BF16)
        r1 = g - g_hi.astype(F32)
        g_mid = r1.astype(BF16)
        g_lo = (r1 - g_mid.astype(F32)).astype(BF16)
        tri_b = tri.astype(BF16)
        gc = _dot(jnp.concatenate([tri_b, tri_b, tri_b], axis=1), jnp.concatenate([g_hi, g_mid, g_lo], axis=0))
        gl = jnp.sum(g, axis=0, keepdims=True)
        beta = jax.nn.sigmoid(ba)
        gate[d] = dict(beta=beta, beta_t=beta.T, gc=gc, gc_t=gc.T, eg=jnp.exp(gc), egl=jnp.exp(gl - gc),
                       eg_last=jnp.exp(gl))

    heads = [(d, h) for d in range(n_streams) for h in range(HEADS)]
    pairs = [(d, t) for d in range(n_streams) for t in range(HEADS // 2)]
    pw_lanes = 2 * CHUNK

    def head(d, part, h, n_heads=1):
        lo = part * D_MODEL + h * HEAD_DIM
        return dirs[d][1][dirs[d][5], :, lo:lo + n_heads * HEAD_DIM]

    def col(d, name, lane_idx):
        return gate[d][name][:, lane_idx:lane_idx + 1]

    def pair_row(d, name, base, t):
        g_t = gate[d][name]
        return jnp.concatenate([g_t[base + 2 * t:base + 2 * t + 1, :], g_t[base + 2 * t + 1:base + 2 * t + 2, :]], axis=1)

    lane_p = lax.broadcasted_iota(jnp.int32, (CHUNK, pw_lanes), 1)
    row_p = lax.broadcasted_iota(jnp.int32, (CHUNK, pw_lanes), 0)
    first_half = lane_p < CHUNK
    col_p = jnp.where(first_half, lane_p, lane_p - CHUNK)
    tri_p = {0: row_p >= col_p, 1: row_p <= col_p}
    strict_p = {0: row_p > col_p, 1: row_p < col_p}
    r2 = lax.broadcasted_iota(jnp.int32, (2 * CHUNK, pw_lanes), 0)
    c2 = lax.broadcasted_iota(jnp.int32, (2 * CHUNK, pw_lanes), 1)
    bd_p = (r2 >= CHUNK) == (c2 >= CHUNK)
    r3 = lax.broadcasted_iota(jnp.int32, (2 * CHUNK, 2 * HEAD_DIM), 0)
    c3 = lax.broadcasted_iota(jnp.int32, (2 * CHUNK, 2 * HEAD_DIM), 1)
    bd_k = (r3 >= CHUNK) == (c3 >= HEAD_DIM)

    def block_diag(pair):
        return jnp.where(bd_p, jnp.concatenate([pair, pair], axis=0), jnp.zeros((), pair.dtype))

    kkqk = {}
    for d, t in pairs:
        k2 = head(d, 1, 2 * t, 2)
        q2 = head(d, 0, 2 * t, 2)
        k_bd = jnp.where(bd_k, jnp.concatenate([k2, k2], axis=0), jnp.zeros((), BF16))
        kkqk[d, t] = lax.dot_general(jnp.concatenate([k2, q2], axis=0), k_bd, (((1,), (1,)), ((), ())),
                                     preferred_element_type=F32)

    decay = {}
    for d, t in pairs:
        ga = HEADS + 2 * t
        g_col = jnp.where(first_half, col(d, "gc", ga), col(d, "gc", ga + 1))
        decay[d, t] = jnp.exp(jnp.where(tri_p[dirs[d][6]], g_col - pair_row(d, "gc_t", HEADS, t), -jnp.inf))
    beta_row = {(d, t): pair_row(d, "beta_t", 0, t) for d, t in pairs}

    eye_p = (row_p == col_p).astype(F32)
    same_block = (row_p // INV_BLOCK) == (col_p // INV_BLOCK)

    def pair_dot(a, b):
        return _dot(a.astype(BF16), block_diag(b.astype(BF16)))

    t_inv = {}
    pw = {}
    n_out = {}
    for i in pairs:
        nm = jnp.where(strict_p[dirs[i[0]][6]], -(kkqk[i][:CHUNK] * decay[i] * beta_row[i]), 0.0)
        n_in = jnp.where(same_block, nm, 0.0)
        n_out[i] = nm - n_in
        t_inv[i] = n_in + eye_p
        pw[i] = pair_dot(n_in, n_in)
    assert CHUNK == 4 * INV_BLOCK
    n_sq = INV_BLOCK.bit_length() - 3
    for j in range(n_sq + 1):
        if j < n_sq:
            x2 = {}
            for i in pairs:
                pb = pw[i].astype(BF16)
                x2[i] = _dot(jnp.concatenate([pb, t_inv[i].astype(BF16)], axis=0), block_diag(pb))
            t_inv = {i: t_inv[i] + x2[i][CHUNK:] for i in pairs}
            pw = {i: x2[i][:CHUNK] for i in pairs}
        else:
            t_inv = {i: t_inv[i] + pair_dot(t_inv[i], pw[i]) for i in pairs}
    m1 = {i: pair_dot(t_inv[i], n_out[i]) for i in pairs}
    m2 = {i: pair_dot(m1[i], m1[i]) for i in pairs}
    g1 = {i: m1[i] + eye_p for i in pairs}
    g2 = {i: g1[i] + pair_dot(g1[i], m2[i]) for i in pairs}
    inv = {i: pair_dot(g2[i], t_inv[i]) for i in pairs}

    sol = {}
    for d, t in pairs:
        rhs = []
        for h in (2 * t, 2 * t + 1):
            k_eg = (head(d, 1, h).astype(F32) * col(d, "eg", HEADS + h)).astype(BF16)
            rhs.append(jnp.concatenate([head(d, 2, h), k_eg], axis=1))
        both = _dot(block_diag(inv[d, t].astype(BF16)), jnp.concatenate(rhs, axis=0))
        sol[d, 2 * t] = both[:CHUNK]
        sol[d, 2 * t + 1] = both[CHUNK:]

    s_old = {(d, h): s_ref[dirs[d][5], dirs[d][6], h] for d, h in heads}
    ws_qs = {}
    for d, h in heads:
        qd = (head(d, 0, h).astype(F32) * col(d, "eg", HEADS + h)).astype(BF16)
        ws_qs[d, h] = _dot(jnp.concatenate([sol[d, h][:, HEAD_DIM:].astype(BF16), qd], axis=0), s_old[d, h].astype(BF16))
    v_hat = {i: (sol[i][:, :HEAD_DIM] - ws_qs[i][:CHUNK]).astype(BF16) for i in heads}
    intra = {}
    for d, t in pairs:
        qkd = (kkqk[d, t][CHUNK:] * decay[d, t] * beta_row[d, t]).astype(BF16)
        both = _dot(block_diag(qkd), jnp.concatenate([v_hat[d, 2 * t], v_hat[d, 2 * t + 1]], axis=0))
        intra[d, 2 * t] = both[:CHUNK]
        intra[d, 2 * t + 1] = both[CHUNK:]
    for d, h in heads:
        ga = HEADS + h
        kd = (head(d, 1, h).astype(F32) * (col(d, "egl", ga) * col(d, "beta", h))).astype(BF16)
        s_ref[dirs[d][5], dirs[d][6], h] = s_old[d, h] * gate[d]["eg_last"][:, ga:ga + 1] + lax.dot_general(
            kd, v_hat[d, h], (((0,), (0,)), ((), ())), preferred_element_type=F32)
        dirs[d][3][dirs[d][5], :, h * HEAD_DIM:(h + 1) * HEAD_DIM] = (ws_qs[d, h][CHUNK:] + intra[d, h]).astype(BF16)


def _deltanet(qkv, ba, gp, nb, nc):
    t_pad = qkv.shape[1]
    width = 3 * D_MODEL
    out = jax.ShapeDtypeStruct((nb, t_pad, D_MODEL), BF16)
    return pl.pallas_call(
        functools.partial(_dn_kernel, nc),
        grid=(nb // DN_BATCH, nc),
        in_specs=[
            pl.BlockSpec((DN_BATCH, CHUNK, width), lambda b, n: (b, n, 0)),
            pl.BlockSpec((DN_BATCH, CHUNK, width), lambda b, n: (b, nc - 1 - n, 0)),
            pl.BlockSpec((DN_BATCH, CHUNK, 128), lambda b, n: (b, n, 0)),
            pl.BlockSpec((DN_BATCH, CHUNK, 128), lambda b, n: (b, nc - 1 - n, 1)),
            pl.BlockSpec((2, 8, 128), lambda b, n: (0, 0, 0)),
        ],
        out_specs=[
            pl.BlockSpec((DN_BATCH, CHUNK, D_MODEL), lambda b, n: (b, n, 0)),
            pl.BlockSpec((DN_BATCH, CHUNK, D_MODEL), lambda b, n: (b, nc - 1 - n, 0)),
        ],
        out_shape=[out, out],
        scratch_shapes=[pltpu.VMEM((DN_BATCH, 2, HEADS, HEAD_DIM, HEAD_DIM), F32)],
        compiler_params=pltpu.CompilerParams(
            dimension_semantics=("parallel", "arbitrary"), vmem_limit_bytes=VMEM_LIMIT),
        name="deltanet",
    )(qkv, qkv, ba, ba, gp)


def _merge_kernel(nt, tm, of_ref, ob_ref, z_ref, scb_ref, scc_ref, scx_ref, sccp_ref, scxp_ref, sccn_ref, scxn_ref,
                  ga_ref, gb_ref, h_ref, nw_ref, cw_ref, wdn_ref, wsc_ref, wout_ref, out_ref, xb_ref):
    is_last = pl.program_id(0) == nt - 1

    hb = BF16_SUBLANES
    xb_ref[0:hb, :] = sccp_ref[...] * scxp_ref[...]
    xb_ref[hb:hb + tm, :] = scc_ref[...] * scx_ref[...]
    xb_ref[hb + tm:, :] = jnp.where(is_last, jnp.zeros((), BF16), sccn_ref[...] * scxn_ref[...])
    win = CHUNK + 2 * hb
    shift = jnp.where(_shift_matrix(SC_CONV)[0], 1.0, 0.0).astype(BF16)

    o = of_ref[...].astype(F32) + ob_ref[...].astype(F32)
    z = z_ref[...].astype(F32)
    parts = []
    for h in range(HEADS):
        sl = slice(h * HEAD_DIM, (h + 1) * HEAD_DIM)
        parts.append((_rms(o[:, sl], nw_ref[...]) * _silu(z[:, sl])).astype(BF16))
    y_a = _dot(jnp.concatenate(parts, axis=1), wdn_ref[...])

    gated = []
    for blk in range(tm // CHUNK):
        x = xb_ref[blk * CHUNK:blk * CHUNK + win, :].reshape(win // hb, hb, D_MODEL)
        conv = _dot(shift, _weighted_taps(x, cw_ref, SC_CONV))
        gated.append(scb_ref[blk * CHUNK:(blk + 1) * CHUNK, :] * conv.astype(BF16))
    y_b = _dot(jnp.concatenate(gated, axis=0), wsc_ref[...])

    merged = (jax.nn.sigmoid(ga_ref[...].astype(F32)) * y_a
              + jax.nn.sigmoid(gb_ref[...].astype(F32)) * y_b)
    out_ref[...] = h_ref[...] + _dot(merged.astype(BF16), wout_ref[...])


def _merge(o_f, o_b, p, hres, nw, cw, wdn, wsc, wout, tm):
    rows = hres.shape[0]
    nt = rows // tm
    hb = BF16_SUBLANES
    per = tm // hb
    n_halo = rows // hb

    def col(c):
        return pl.BlockSpec((tm, D_MODEL), lambda i: (i, c))

    def prev(c):
        return pl.BlockSpec((hb, D_MODEL), lambda i: (jnp.maximum(i * per - 1, 0), c))

    def nxt(c):
        return pl.BlockSpec((hb, D_MODEL), lambda i: (jnp.minimum(i * per + per, n_halo - 1), c))

    def full(shape):
        return pl.BlockSpec(shape, lambda i: (0,) * len(shape))

    return pl.pallas_call(
        functools.partial(_merge_kernel, nt, tm),
        grid=(nt,),
        in_specs=[
            col(0), col(0),
            col(3), col(4), col(5), col(6), prev(5), prev(6), nxt(5), nxt(6), col(7), col(8),
            pl.BlockSpec((tm, D_MODEL), lambda i: (i, 0)),
            full((1, HEAD_DIM)), full((SC_CONV, hb, D_MODEL)),
            full((D_MODEL, D_MODEL)), full((D_MODEL, D_MODEL)), full((D_MODEL, D_MODEL)),
        ],
        out_specs=pl.BlockSpec((tm, D_MODEL), lambda i: (i, 0)),
        out_shape=jax.ShapeDtypeStruct((rows, D_MODEL), F32),
        scratch_shapes=[pltpu.VMEM((tm + 2 * hb, D_MODEL), BF16)],
        input_output_aliases={12: 0},
        compiler_params=pltpu.CompilerParams(
            dimension_semantics=("parallel",), vmem_limit_bytes=VMEM_LIMIT),
        name="merge",
    )(o_f, o_b, p, p, p, p, p, p, p, p, p, p, hres, nw, cw, wdn, wsc, wout)


def _mlp_kernel(final, x_ref, nw_ref, wg_ref, wu_ref, wd_ref, fw_ref, out_ref):
    x = x_ref[...].reshape(x_ref.shape[-2:])
    h2 = _rms(x, nw_ref[...]).astype(BF16)
    acc = x
    for lo in range(0, D_FF, FF_CHUNK):
        sl = slice(lo, lo + FF_CHUNK)
        a = _silu(_dot(h2, wg_ref[:, sl])) * _dot(h2, wu_ref[:, sl])
        acc = acc + _dot(a.astype(BF16), wd_ref[sl, :])
    if final:
        acc = _rms(acc, fw_ref[...])
    out_ref[...] = acc


def _mlp(x, nw, wg, wu, wd, fw, tm):
    rows = x.shape[0]

    def full(shape):
        return pl.BlockSpec(shape, lambda i: (0,) * len(shape))

    return pl.pallas_call(
        functools.partial(_mlp_kernel, False),
        grid=(rows // tm,),
        in_specs=[
            pl.BlockSpec((tm, D_MODEL), lambda i: (i, 0)),
            full((1, D_MODEL)), full((D_MODEL, D_FF)), full((D_MODEL, D_FF)), full((D_FF, D_MODEL)),
            full((1, D_MODEL)),
        ],
        out_specs=pl.BlockSpec((tm, D_MODEL), lambda i: (i, 0)),
        out_shape=jax.ShapeDtypeStruct((rows, D_MODEL), F32),
        input_output_aliases={0: 0},
        compiler_params=pltpu.CompilerParams(
            dimension_semantics=("parallel",), vmem_limit_bytes=VMEM_LIMIT),
        name="mlp",
    )(x, nw, wg, wu, wd, fw)


def _mlp_final(x, nw, wg, wu, wd, fw, nb, seq, tm):
    t_pad = x.shape[0] // nb
    front = t_pad - seq

    def full(shape):
        return pl.BlockSpec(shape, lambda b, j: (0,) * len(shape))

    return pl.pallas_call(
        functools.partial(_mlp_kernel, True),
        grid=(nb, seq // tm),
        in_specs=[
            pl.BlockSpec((pl.Element(tm), pl.Element(D_MODEL)),
                         lambda b, j: (pl.multiple_of(b * t_pad + front + j * tm, 8), 0)),
            full((1, D_MODEL)), full((D_MODEL, D_FF)), full((D_MODEL, D_FF)), full((D_FF, D_MODEL)),
            full((1, D_MODEL)),
        ],
        out_specs=pl.BlockSpec((None, tm, D_MODEL), lambda b, j: (b, j, 0)),
        out_shape=jax.ShapeDtypeStruct((nb, seq, D_MODEL), F32),
        compiler_params=pltpu.CompilerParams(
            dimension_semantics=("parallel", "parallel"), vmem_limit_bytes=VMEM_LIMIT),
        name="mlp_final",
    )(x, nw, wg, wu, wd, fw)


def _row_tile(rows, want, step=BF16_SUBLANES):
    t = want - want % step
    while rows % t:
        t -= step
    return t


def kernel(x, meta_tokens, norm1_w, w_in, dn_conv_w, A_log, dt_bias, dn_norm_w, sc_conv_w, w_branch_dn,
           w_branch_sc, w_out, norm2_w, w_gate_up, w_down, final_norm_w):
    nb, seq, _ = x.shape
    depth = w_in.shape[0]
    t_pad = PAD_FRONT + N_META + seq
    nc = t_pad // CHUNK
    rows = nb * t_pad

    meta = jnp.broadcast_to(meta_tokens[None].astype(x.dtype), (nb, N_META, D_MODEL))
    hres = jnp.concatenate([jnp.zeros((nb, PAD_FRONT, D_MODEL), x.dtype), meta, x], axis=1)
    hres = hres.reshape(rows, D_MODEL)

    q_cols = 4 * D_MODEL
    ba_cols = 4 * HEADS
    tm_in = _row_tile(rows, 1536)
    tm_prep = _row_tile(rows, 512, CHUNK)
    tm_merge = _row_tile(rows, 512, CHUNK)
    tm_mlp = _row_tile(rows, 512)

    for l in range(depth):
        w_main = jnp.concatenate([w_in[l, :, :q_cols], w_in[l, :, q_cols + ba_cols:]], axis=1).astype(BF16)
        w_b = w_in[l, :, q_cols:q_cols + 2 * HEADS]
        w_a = w_in[l, :, q_cols + 2 * HEADS:q_cols + ba_cols]
        zpad = jnp.zeros((D_MODEL, 128 - 2 * HEADS), w_in.dtype)
        w_ba = jnp.concatenate([w_b[:, :HEADS], w_a[:, :HEADS], zpad,
                                w_b[:, HEADS:], w_a[:, HEADS:], zpad], axis=1).astype(BF16)
        p, ba = _inproj(hres, norm1_w[l][None], w_main, w_ba, tm_in, 1536)

        cw = jnp.broadcast_to(dn_conv_w[l].astype(BF16)[:, None, :], (DN_CONV, BF16_SUBLANES, 3 * D_MODEL))
        qkv = _prep(p, cw, tm_prep, t_pad)
        lane_pad = ((0, 0), (HEADS, 128 - 2 * HEADS))
        gp = jnp.stack([jnp.pad(A_log[l], lane_pad), jnp.pad(dt_bias[l], lane_pad)], axis=1)
        gp = jnp.pad(gp, ((0, 0), (0, 6), (0, 0))).astype(F32)
        o_f, o_b = _deltanet(qkv.reshape(nb, t_pad, 3 * D_MODEL), ba.reshape(nb, t_pad, 256), gp, nb, nc)
        o_f = o_f.reshape(rows, D_MODEL)
        o_b = o_b.reshape(rows, D_MODEL)

        sc_cw = jnp.broadcast_to(sc_conv_w[l].astype(BF16)[:, None, :], (SC_CONV, BF16_SUBLANES, D_MODEL))
        hres = _merge(o_f, o_b, p, hres, dn_norm_w[l][None], sc_cw,
                      w_branch_dn[l].astype(BF16), w_branch_sc[l].astype(BF16), w_out[l].astype(BF16), tm_merge)

        mlp_w = (norm2_w[l][None], w_gate_up[l, :, :D_FF].astype(BF16), w_gate_up[l, :, D_FF:].astype(BF16),
                 w_down[l].astype(BF16), final_norm_w[None])
        if l < depth - 1:
            hres = _mlp(hres, *mlp_w, tm_mlp)
        else:
            out = _mlp_final(hres, *mlp_w, nb, seq, _row_tile(seq, 512))
    return out
```

```python
import functools

import jax
import jax.numpy as jnp
from jax import lax
from jax.experimental import pallas as pl
from jax.experimental.pallas import tpu as pltpu

F32 = jnp.float32
BF16 = jnp.bfloat16

D_MODEL = 1024
N_META = 16
CHUNK = 64
PAD_FRONT = (-N_META) % CHUNK
HEADS = 8
HEAD_DIM = 128
DN_CONV = 5
SC_CONV = 3
D_FF = 2816
RMS_EPS = 1e-6
L2_EPS = 1e-6

DN_BATCH = 8
INV_BLOCK = 16
FF_CHUNK = 256
BF16_SUBLANES = 16
VMEM_LIMIT = 56 * 1024 * 1024


def _silu(x):
    return x * jax.nn.sigmoid(x)


def _rms(x, w):
    return x * lax.rsqrt(jnp.mean(x * x, axis=-1, keepdims=True) + RMS_EPS) * w


def _dot(a, b):
    return jnp.dot(a, b, preferred_element_type=F32)


def _inproj_kernel(x_ref, nw_ref, w_ref, wba_ref, p_ref, ba_ref, h_ref):
    @pl.when(pl.program_id(1) == 0)
    def _():
        h = _rms(x_ref[...], nw_ref[...]).astype(BF16)
        h_ref[...] = h
        ba_ref[...] = _dot(h, wba_ref[...])

    p_ref[...] = _dot(h_ref[...], w_ref[...]).astype(BF16)


def _inproj(x, nw, w, wba, tm, tn):
    rows, n = x.shape[0], w.shape[1]
    return pl.pallas_call(
        _inproj_kernel,
        grid=(rows // tm, n // tn),
        in_specs=[
            pl.BlockSpec((tm, D_MODEL), lambda i, j: (i, 0)),
            pl.BlockSpec((1, D_MODEL), lambda i, j: (0, 0)),
            pl.BlockSpec((D_MODEL, tn), lambda i, j: (0, j)),
            pl.BlockSpec((D_MODEL, 256), lambda i, j: (0, 0)),
        ],
        out_specs=[
            pl.BlockSpec((tm, tn), lambda i, j: (i, j)),
            pl.BlockSpec((tm, 256), lambda i, j: (i, 0)),
        ],
        out_shape=[
            jax.ShapeDtypeStruct((rows, n), BF16),
            jax.ShapeDtypeStruct((rows, 256), F32),
        ],
        scratch_shapes=[pltpu.VMEM((tm, D_MODEL), BF16)],
        compiler_params=pltpu.CompilerParams(
            dimension_semantics=("parallel", "arbitrary"), vmem_limit_bytes=VMEM_LIMIT),
        name="inproj",
    )(x, nw, w, wba)


def _tap_groups(n_taps):
    hb = BF16_SUBLANES
    r = n_taps // 2
    return [((hb - r + s) // hb, (hb - r + s + CHUNK - 1) // hb - (hb - r + s) // hb + 1) for s in range(n_taps)]


def _shift_matrix(n_taps):
    hb = BF16_SUBLANES
    r = n_taps // 2
    groups = _tap_groups(n_taps)
    k = sum(n for _, n in groups) * hb
    ri = lax.broadcasted_iota(jnp.int32, (CHUNK, k), 0)
    ci = lax.broadcasted_iota(jnp.int32, (CHUNK, k), 1)
    hit = jnp.zeros((CHUNK, k), jnp.bool_)
    base = 0
    for s, (first, n) in enumerate(groups):
        hit = jnp.logical_or(hit, ci == ri + (base + hb - r + s - first * hb))
        base += n * hb
    return hit, ri


def _weighted_taps(x, cw_ref, n_taps):
    taps = []
    for s, (first, n) in enumerate(_tap_groups(n_taps)):
        taps.append((x[first:first + n] * cw_ref[s][None]).reshape(n * BF16_SUBLANES, x.shape[-1]))
    return jnp.concatenate(taps, axis=0)


def _prep_kernel(nt, tm, t_pad, pm_ref, pp_ref, pn_ref, cw_ref, o_ref, xb_ref):
    i = pl.program_id(0)
    hb = BF16_SUBLANES
    width = 3 * D_MODEL
    xb_ref[0:hb, :] = pp_ref[...]
    xb_ref[hb:hb + tm, :] = pm_ref[...]
    xb_ref[hb + tm:, :] = jnp.where(i == nt - 1, jnp.zeros((), BF16), pn_ref[...])

    win = CHUNK + 2 * hb
    shift, ri = _shift_matrix(DN_CONV)
    for blk in range(tm // CHUNK):
        x = xb_ref[blk * CHUNK:blk * CHUNK + win, :].reshape(win // hb, hb, width)
        taps = _weighted_taps(x, cw_ref, DN_CONV)
        pos = lax.rem(i * tm + blk * CHUNK, t_pad) + ri
        pos = jnp.where(pos >= t_pad, pos - t_pad, pos)
        shift_v = jnp.where(jnp.logical_and(shift, pos >= PAD_FRONT), 1.0, 0.0).astype(BF16)
        qkv = _silu(_dot(shift_v, taps))

        rows = slice(blk * CHUNK, (blk + 1) * CHUNK)
        for h in range(HEADS):
            for part in range(3):
                sl = slice(part * D_MODEL + h * HEAD_DIM, part * D_MODEL + (h + 1) * HEAD_DIM)
                t = qkv[:, sl]
                if part == 0:
                    t = t * (lax.rsqrt(jnp.sum(t * t, axis=-1, keepdims=True) + L2_EPS) * (HEAD_DIM ** -0.5))
                elif part == 1:
                    t = t * lax.rsqrt(jnp.sum(t * t, axis=-1, keepdims=True) + L2_EPS)
                o_ref[rows, sl] = t.astype(BF16)


def _prep(p, cw, tm, t_pad):
    rows = p.shape[0]
    nt = rows // tm
    hb = BF16_SUBLANES
    per = tm // hb
    n_halo = rows // hb
    width = 3 * D_MODEL
    return pl.pallas_call(
        functools.partial(_prep_kernel, nt, tm, t_pad),
        grid=(nt,),
        in_specs=[
            pl.BlockSpec((tm, width), lambda i: (i, 0)),
            pl.BlockSpec((hb, width), lambda i: (jnp.maximum(i * per - 1, 0), 0)),
            pl.BlockSpec((hb, width), lambda i: (jnp.minimum(i * per + per, n_halo - 1), 0)),
            pl.BlockSpec((DN_CONV, hb, width), lambda i: (0, 0, 0)),
        ],
        out_specs=pl.BlockSpec((tm, width), lambda i: (i, 0)),
        out_shape=jax.ShapeDtypeStruct((rows, width), BF16),
        scratch_shapes=[pltpu.VMEM((tm + 2 * hb, width), BF16)],
        compiler_params=pltpu.CompilerParams(
            dimension_semantics=("parallel",), vmem_limit_bytes=VMEM_LIMIT),
        name="qkvprep",
    )(p, p, p, cw)


def _dn_kernel(nc, qf_ref, qb_ref, baf_ref, bab_ref, gp_ref, of_ref, ob_ref, s_ref):
    n = pl.program_id(1)

    @pl.when(n == 0)
    def _():
        s_ref[...] = jnp.zeros_like(s_ref)

    rid = lax.broadcasted_iota(jnp.int32, (CHUNK, 1), 0)
    ri = lax.broadcasted_iota(jnp.int32, (CHUNK, CHUNK), 0)
    ci = lax.broadcasted_iota(jnp.int32, (CHUNK, CHUNK), 1)
    lane = lax.broadcasted_iota(jnp.int32, (1, 128), 1)
    a_lanes = jnp.logical_and(lane >= HEADS, lane < 2 * HEADS)
    dirs = []
    for bi in range(DN_BATCH):
        dirs.append((2 * bi, qf_ref, baf_ref, of_ref, n, bi, 0))
        dirs.append((2 * bi + 1, qb_ref, bab_ref, ob_ref, nc - 1 - n, bi, 1))
    n_streams = len(dirs)

    gate = {}
    for d, _, ba_ref, _, c, bi, dirn in dirs:
        ba = ba_ref[bi]
        valid = jnp.logical_or(c > 0, rid >= PAD_FRONT).astype(F32)
        neg_a = jnp.where(a_lanes, -jnp.exp(gp_ref[dirn, 0:1, :]), 0.0)
        g = neg_a * jax.nn.softplus(ba + gp_ref[dirn, 1:2, :]) * valid
        tri = (ri >= ci) if dirn == 0 else (ri <= ci)
        g_hi = g.astype(BF16)
        r1 = g - g_hi.astype(F32)
        g_mid = r1.astype(BF16)
        g_lo = (r1 - g_mid.astype(F32)).astype(BF16)
        tri_b = tri.astype(BF16)
        gc = _dot(jnp.concatenate([tri_b, tri_b, tri_b], axis=1), jnp.concatenate([g_hi, g_mid, g_lo], axis=0))
        gl = jnp.sum(g, axis=0, keepdims=True)
        beta = jax.nn.sigmoid(ba)
        gate[d] = dict(beta=beta, beta_t=beta.T, gc=gc, gc_t=gc.T, eg=jnp.exp(gc), egl=jnp.exp(gl - gc),
                       eg_last=jnp.exp(gl))

    heads = [(d, h) for d in range(n_streams) for h in range(HEADS)]
    pairs = [(d, t) for d in range(n_streams) for t in range(HEADS // 2)]
    pw_lanes = 2 * CHUNK

    def head(d, part, h, n_heads=1):
        lo = part * D_MODEL + h * HEAD_DIM
        return dirs[d][1][dirs[d][5], :, lo:lo + n_heads * HEAD_DIM]

    def col(d, name, lane_idx):
        return gate[d][name][:, lane_idx:lane_idx + 1]

    def pair_row(d, name, base, t):
        g_t = gate[d][name]
        return jnp.concatenate([g_t[base + 2 * t:base + 2 * t + 1, :], g_t[base + 2 * t + 1:base + 2 * t + 2, :]], axis=1)

    lane_p = lax.broadcasted_iota(jnp.int32, (CHUNK, pw_lanes), 1)
    row_p = lax.broadcasted_iota(jnp.int32, (CHUNK, pw_lanes), 0)
    first_half = lane_p < CHUNK
    col_p = jnp.where(first_half, lane_p, lane_p - CHUNK)
    tri_p = {0: row_p >= col_p, 1: row_p <= col_p}
    strict_p = {0: row_p > col_p, 1: row_p < col_p}
    r2 = lax.broadcasted_iota(jnp.int32, (2 * CHUNK, pw_lanes), 0)
    c2 = lax.broadcasted_iota(jnp.int32, (2 * CHUNK, pw_lanes), 1)
    bd_p = (r2 >= CHUNK) == (c2 >= CHUNK)
    r3 = lax.broadcasted_iota(jnp.int32, (2 * CHUNK, 2 * HEAD_DIM), 0)
    c3 = lax.broadcasted_iota(jnp.int32, (2 * CHUNK, 2 * HEAD_DIM), 1)
    bd_k = (r3 >= CHUNK) == (c3 >= HEAD_DIM)

    def block_diag(pair):
        return jnp.where(bd_p, jnp.concatenate([pair, pair], axis=0), jnp.zeros((), pair.dtype))

    kkqk = {}
    for d, t in pairs:
        k2 = head(d, 1, 2 * t, 2)
        q2 = head(d, 0, 2 * t, 2)
        k_bd = jnp.where(bd_k, jnp.concatenate([k2, k2], axis=0), jnp.zeros((), BF16))
        kkqk[d, t] = lax.dot_general(jnp.concatenate([k2, q2], axis=0), k_bd, (((1,), (1,)), ((), ())),
                                     preferred_element_type=F32)

    decay = {}
    for d, t in pairs:
        ga = HEADS + 2 * t
        g_col = jnp.where(first_half, col(d, "gc", ga), col(d, "gc", ga + 1))
        decay[d, t] = jnp.exp(jnp.where(tri_p[dirs[d][6]], g_col - pair_row(d, "gc_t", HEADS, t), -jnp.inf))
    beta_row = {(d, t): pair_row(d, "beta_t", 0, t) for d, t in pairs}

    eye_p = (row_p == col_p).astype(F32)
    same_block = (row_p // INV_BLOCK) == (col_p // INV_BLOCK)

    def pair_dot(a, b):
        return _dot(a.astype(BF16), block_diag(b.astype(BF16)))

    t_inv = {}
    pw = {}
    n_out = {}
    for i in pairs:
        nm = jnp.where(strict_p[dirs[i[0]][6]], -(kkqk[i][:CHUNK] * decay[i] * beta_row[i]), 0.0)
        n_in = jnp.where(same_block, nm, 0.0)
        n_out[i] = nm - n_in
        t_inv[i] = n_in + eye_p
        pw[i] = pair_dot(n_in, n_in)
    assert CHUNK == 4 * INV_BLOCK
    n_sq = INV_BLOCK.bit_length() - 3
    for j in range(n_sq + 1):
        if j < n_sq:
            x2 = {}
            for i in pairs:
                pb = pw[i].astype(BF16)
                x2[i] = _dot(jnp.concatenate([pb, t_inv[i].astype(BF16)], axis=0), block_diag(pb))
            t_inv = {i: t_inv[i] + x2[i][CHUNK:] for i in pairs}
            pw = {i: x2[i][:CHUNK] for i in pairs}
        else:
            t_inv = {i: t_inv[i] + pair_dot(t_inv[i], pw[i]) for i in pairs}
    m1 = {i: pair_dot(t_inv[i], n_out[i]) for i in pairs}
    m2 = {i: pair_dot(m1[i], m1[i]) for i in pairs}
    g1 = {i: m1[i] + eye_p for i in pairs}
    g2 = {i: g1[i] + pair_dot(g1[i], m2[i]) for i in pairs}
    inv = {i: pair_dot(g2[i], t_inv[i]) for i in pairs}

    sol = {}
    for d, t in pairs:
        rhs = []
        for h in (2 * t, 2 * t + 1):
            k_eg = (head(d, 1, h).astype(F32) * col(d, "eg", HEADS + h)).astype(BF16)
            rhs.append(jnp.concatenate([head(d, 2, h), k_eg], axis=1))
        both = _dot(block_diag(inv[d, t].astype(BF16)), jnp.concatenate(rhs, axis=0))
        sol[d, 2 * t] = both[:CHUNK]
        sol[d, 2 * t + 1] = both[CHUNK:]

    s_old = {(d, h): s_ref[dirs[d][5], dirs[d][6], h] for d, h in heads}
    ws_qs = {}
    for d, h in heads:
        qd = (head(d, 0, h).astype(F32) * col(d, "eg", HEADS + h)).astype(BF16)
        ws_qs[d, h] = _dot(jnp.concatenate([sol[d, h][:, HEAD_DIM:].astype(BF16), qd], axis=0), s_old[d, h].astype(BF16))
    v_hat = {i: (sol[i][:, :HEAD_DIM] - ws_qs[i][:CHUNK]).astype(BF16) for i in heads}
    intra = {}
    for d, t in pairs:
        qkd = (kkqk[d, t][CHUNK:] * decay[d, t] * beta_row[d, t]).astype(BF16)
        both = _dot(block_diag(qkd), jnp.concatenate([v_hat[d, 2 * t], v_hat[d, 2 * t + 1]], axis=0))
        intra[d, 2 * t] = both[:CHUNK]
        intra[d, 2 * t + 1] = both[CHUNK:]
    for d, h in heads:
        ga = HEADS + h
        kd = (head(d, 1, h).astype(F32) * (col(d, "egl", ga) * col(d, "beta", h))).astype(BF16)
        s_ref[dirs[d][5], dirs[d][6], h] = s_old[d, h] * gate[d]["eg_last"][:, ga:ga + 1] + lax.dot_general(
            kd, v_hat[d, h], (((0,), (0,)), ((), ())), preferred_element_type=F32)
        dirs[d][3][dirs[d][5], :, h * HEAD_DIM:(h + 1) * HEAD_DIM] = (ws_qs[d, h][CHUNK:] + intra[d, h]).astype(BF16)


def _deltanet(qkv, ba, gp, nb, nc):
    t_pad = qkv.shape[1]
    width = 3 * D_MODEL
    out = jax.ShapeDtypeStruct((nb, t_pad, D_MODEL), BF16)
    return pl.pallas_call(
        functools.partial(_dn_kernel, nc),
        grid=(nb // DN_BATCH, nc),
        in_specs=[
            pl.BlockSpec((DN_BATCH, CHUNK, width), lambda b, n: (b, n, 0)),
            pl.BlockSpec((DN_BATCH, CHUNK, width), lambda b, n: (b, nc - 1 - n, 0)),
            pl.BlockSpec((DN_BATCH, CHUNK, 128), lambda b, n: (b, n, 0)),
            pl.BlockSpec((DN_BATCH, CHUNK, 128), lambda b, n: (b, nc - 1 - n, 1)),
            pl.BlockSpec((2, 8, 128), lambda b, n: (0, 0, 0)),
        ],
        out_specs=[
            pl.BlockSpec((DN_BATCH, CHUNK, D_MODEL), lambda b, n: (b, n, 0)),
            pl.BlockSpec((DN_BATCH, CHUNK, D_MODEL), lambda b, n: (b, nc - 1 - n, 0)),
        ],
        out_shape=[out, out],
        scratch_shapes=[pltpu.VMEM((DN_BATCH, 2, HEADS, HEAD_DIM, HEAD_DIM), F32)],
        compiler_params=pltpu.CompilerParams(
            dimension_semantics=("parallel", "arbitrary"), vmem_limit_bytes=VMEM_LIMIT),
        name="deltanet",
    )(qkv, qkv, ba, ba, gp)


def _merge_kernel(nt, tm, of_ref, ob_ref, z_ref, scb_ref, scc_ref, scx_ref, sccp_ref, scxp_ref, sccn_ref, scxn_ref,
                  ga_ref, gb_ref, h_ref, nw_ref, cw_ref, wdn_ref, wsc_ref, wout_ref, out_ref, xb_ref):
    is_last = pl.program_id(0) == nt - 1

    hb = BF16_SUBLANES
    xb_ref[0:hb, :] = sccp_ref[...] * scxp_ref[...]
    xb_ref[hb:hb + tm, :] = scc_ref[...] * scx_ref[...]
    xb_ref[hb + tm:, :] = jnp.where(is_last, jnp.zeros((), BF16), sccn_ref[...] * scxn_ref[...])
    win = CHUNK + 2 * hb
    shift = jnp.where(_shift_matrix(SC_CONV)[0], 1.0, 0.0).astype(BF16)

    gated = []
    for blk in range(tm // CHUNK):
        x = xb_ref[blk * CHUNK:blk * CHUNK + win, :].reshape(win // hb, hb, D_MODEL)
        conv = _dot(shift, _weighted_taps(x, cw_ref, SC_CONV))
        gated.append(scb_ref[blk * CHUNK:(blk + 1) * CHUNK, :] * conv.astype(BF16))
    y_b = _dot(jnp.concatenate(gated, axis=0), wsc_ref[...])
    mix_b = jax.nn.sigmoid(gb_ref[...].astype(F32)) * y_b

    y_a = jnp.zeros((tm, D_MODEL), F32)
    for lo in range(0, D_MODEL, 2 * HEAD_DIM):
        cols = slice(lo, lo + 2 * HEAD_DIM)
        o = of_ref[:, cols].astype(F32) + ob_ref[:, cols].astype(F32)
        z = z_ref[:, cols].astype(F32)
        parts = []
        for h0 in (0, HEAD_DIM):
            sl = slice(h0, h0 + HEAD_DIM)
            parts.append((_rms(o[:, sl], nw_ref[...]) * _silu(z[:, sl])).astype(BF16))
        y_a = y_a + _dot(jnp.concatenate(parts, axis=1), wdn_ref[cols, :])

    merged = jax.nn.sigmoid(ga_ref[...].astype(F32)) * y_a + mix_b
    out_ref[...] = h_ref[...] + _dot(merged.astype(BF16), wout_ref[...])


def _merge(o_f, o_b, p, hres, nw, cw, wdn, wsc, wout, tm):
    rows = hres.shape[0]
    nt = rows // tm
    hb = BF16_SUBLANES
    per = tm // hb
    n_halo = rows // hb

    def col(c):
        return pl.BlockSpec((tm, D_MODEL), lambda i: (i, c))

    def prev(c):
        return pl.BlockSpec((hb, D_MODEL), lambda i: (jnp.maximum(i * per - 1, 0), c))

    def nxt(c):
        return pl.BlockSpec((hb, D_MODEL), lambda i: (jnp.minimum(i * per + per, n_halo - 1), c))

    def full(shape):
        return pl.BlockSpec(shape, lambda i: (0,) * len(shape))

    return pl.pallas_call(
        functools.partial(_merge_kernel, nt, tm),
        grid=(nt,),
        in_specs=[
            col(0), col(0),
            col(3), col(4), col(5), col(6), prev(5), prev(6), nxt(5), nxt(6), col(7), col(8),
            pl.BlockSpec((tm, D_MODEL), lambda i: (i, 0)),
            full((1, HEAD_DIM)), full((SC_CONV, hb, D_MODEL)),
            full((D_MODEL, D_MODEL)), full((D_MODEL, D_MODEL)), full((D_MODEL, D_MODEL)),
        ],
        out_specs=pl.BlockSpec((tm, D_MODEL), lambda i: (i, 0)),
        out_shape=jax.ShapeDtypeStruct((rows, D_MODEL), F32),
        scratch_shapes=[pltpu.VMEM((tm + 2 * hb, D_MODEL), BF16)],
        input_output_aliases={12: 0},
        compiler_params=pltpu.CompilerParams(
            dimension_semantics=("parallel",), vmem_limit_bytes=VMEM_LIMIT),
        name="merge",
    )(o_f, o_b, p, p, p, p, p, p, p, p, p, p, hres, nw, cw, wdn, wsc, wout)


def _mlp_kernel(final, x_ref, nw_ref, wg_ref, wu_ref, wd_ref, fw_ref, out_ref):
    x = x_ref[...].reshape(x_ref.shape[-2:])
    h2 = _rms(x, nw_ref[...]).astype(BF16)
    acc = x
    for lo in range(0, D_FF, FF_CHUNK):
        sl = slice(lo, lo + FF_CHUNK)
        a = _silu(_dot(h2, wg_ref[:, sl])) * _dot(h2, wu_ref[:, sl])
        acc = acc + _dot(a.astype(BF16), wd_ref[sl, :])
    if final:
        acc = _rms(acc, fw_ref[...])
    out_ref[...] = acc


def _mlp(x, nw, wg, wu, wd, fw, tm):
    rows = x.shape[0]

    def full(shape):
        return pl.BlockSpec(shape, lambda i: (0,) * len(shape))

    return pl.pallas_call(
        functools.partial(_mlp_kernel, False),
        grid=(rows // tm,),
        in_specs=[
            pl.BlockSpec((tm, D_MODEL), lambda i: (i, 0)),
            full((1, D_MODEL)), full((D_MODEL, D_FF)), full((D_MODEL, D_FF)), full((D_FF, D_MODEL)),
            full((1, D_MODEL)),
        ],
        out_specs=pl.BlockSpec((tm, D_MODEL), lambda i: (i, 0)),
        out_shape=jax.ShapeDtypeStruct((rows, D_MODEL), F32),
        input_output_aliases={0: 0},
        compiler_params=pltpu.CompilerParams(
            dimension_semantics=("parallel",), vmem_limit_bytes=VMEM_LIMIT),
        name="mlp",
    )(x, nw, wg, wu, wd, fw)


def _mlp_final(x, nw, wg, wu, wd, fw, nb, seq, tm):
    t_pad = x.shape[0] // nb
    front = t_pad - seq

    def full(shape):
        return pl.BlockSpec(shape, lambda b, j: (0,) * len(shape))

    return pl.pallas_call(
        functools.partial(_mlp_kernel, True),
        grid=(nb, seq // tm),
        in_specs=[
            pl.BlockSpec((pl.Element(tm), pl.Element(D_MODEL)),
                         lambda b, j: (pl.multiple_of(b * t_pad + front + j * tm, 8), 0)),
            full((1, D_MODEL)), full((D_MODEL, D_FF)), full((D_MODEL, D_FF)), full((D_FF, D_MODEL)),
            full((1, D_MODEL)),
        ],
        out_specs=pl.BlockSpec((None, tm, D_MODEL), lambda b, j: (b, j, 0)),
        out_shape=jax.ShapeDtypeStruct((nb, seq, D_MODEL), F32),
        compiler_params=pltpu.CompilerParams(
            dimension_semantics=("parallel", "parallel"), vmem_limit_bytes=VMEM_LIMIT),
        name="mlp_final",
    )(x, nw, wg, wu, wd, fw)


def _row_tile(rows, want, step=BF16_SUBLANES):
    t = want - want % step
    while rows % t:
        t -= step
    return t


def kernel(x, meta_tokens, norm1_w, w_in, dn_conv_w, A_log, dt_bias, dn_norm_w, sc_conv_w, w_branch_dn,
           w_branch_sc, w_out, norm2_w, w_gate_up, w_down, final_norm_w):
    nb, seq, _ = x.shape
    depth = w_in.shape[0]
    t_pad = PAD_FRONT + N_META + seq
    nc = t_pad // CHUNK
    rows = nb * t_pad

    meta = jnp.broadcast_to(meta_tokens[None].astype(x.dtype), (nb, N_META, D_MODEL))
    hres = jnp.concatenate([jnp.zeros((nb, PAD_FRONT, D_MODEL), x.dtype), meta, x], axis=1)
    hres = hres.reshape(rows, D_MODEL)

    q_cols = 4 * D_MODEL
    ba_cols = 4 * HEADS
    tm_in = _row_tile(rows, 1536)
    tm_prep = _row_tile(rows, 512, CHUNK)
    tm_merge = _row_tile(rows, 512, CHUNK)
    tm_mlp = _row_tile(rows, 512)

    for l in range(depth):
        w_main = jnp.concatenate([w_in[l, :, :q_cols], w_in[l, :, q_cols + ba_cols:]], axis=1).astype(BF16)
        w_b = w_in[l, :, q_cols:q_cols + 2 * HEADS]
        w_a = w_in[l, :, q_cols + 2 * HEADS:q_cols + ba_cols]
        zpad = jnp.zeros((D_MODEL, 128 - 2 * HEADS), w_in.dtype)
        w_ba = jnp.concatenate([w_b[:, :HEADS], w_a[:, :HEADS], zpad,
                                w_b[:, HEADS:], w_a[:, HEADS:], zpad], axis=1).astype(BF16)
        p, ba = _inproj(hres, norm1_w[l][None], w_main, w_ba, tm_in, 1536)

        cw = jnp.broadcast_to(dn_conv_w[l].astype(BF16)[:, None, :], (DN_CONV, BF16_SUBLANES, 3 * D_MODEL))
        qkv = _prep(p, cw, tm_prep, t_pad)
        lane_pad = ((0, 0), (HEADS, 128 - 2 * HEADS))
        gp = jnp.stack([jnp.pad(A_log[l], lane_pad), jnp.pad(dt_bias[l], lane_pad)], axis=1)
        gp = jnp.pad(gp, ((0, 0), (0, 6), (0, 0))).astype(F32)
        o_f, o_b = _deltanet(qkv.reshape(nb, t_pad, 3 * D_MODEL), ba.reshape(nb, t_pad, 256), gp, nb, nc)
        o_f = o_f.reshape(rows, D_MODEL)
        o_b = o_b.reshape(rows, D_MODEL)

        sc_cw = jnp.broadcast_to(sc_conv_w[l].astype(BF16)[:, None, :], (SC_CONV, BF16_SUBLANES, D_MODEL))
        hres = _merge(o_f, o_b, p, hres, dn_norm_w[l][None], sc_cw,
                      w_branch_dn[l].astype(BF16), w_branch_sc[l].astype(BF16), w_out[l].astype(BF16), tm_merge)

        mlp_w = (norm2_w[l][None], w_gate_up[l, :, :D_FF].astype(BF16), w_gate_up[l, :, D_FF:].astype(BF16),
                 w_down[l].astype(BF16), final_norm_w[None])
        if l < depth - 1:
            hres = _mlp(hres, *mlp_w, tm_mlp)
        else:
            out = _mlp_final(hres, *mlp_w, nb, seq, _row_tile(seq, 512))
    return out
```

```python
import functools

import jax
import jax.numpy as jnp
from jax import lax
from jax.experimental import pallas as pl
from jax.experimental.pallas import tpu as pltpu

F32 = jnp.float32
BF16 = jnp.bfloat16

D_MODEL = 1024
N_META = 16
CHUNK = 64
PAD_FRONT = (-N_META) % CHUNK
HEADS = 8
HEAD_DIM = 128
DN_CONV = 5
SC_CONV = 3
D_FF = 2816
RMS_EPS = 1e-6
L2_EPS = 1e-6

DN_BATCH = 8
INV_BLOCK = 16
FF_CHUNK = 256
BF16_SUBLANES = 16
VMEM_LIMIT = 56 * 1024 * 1024


def _silu(x):
    return x * jax.nn.sigmoid(x)


def _rms(x, w):
    return x * lax.rsqrt(jnp.mean(x * x, axis=-1, keepdims=True) + RMS_EPS) * w


def _dot(a, b):
    return jnp.dot(a, b, preferred_element_type=F32)


def _inproj_kernel(x_ref, nw_ref, w_ref, wba_ref, p_ref, ba_ref, h_ref):
    @pl.when(pl.program_id(1) == 0)
    def _():
        h = _rms(x_ref[...], nw_ref[...]).astype(BF16)
        h_ref[...] = h
        ba_ref[...] = _dot(h, wba_ref[...])

    p_ref[...] = _dot(h_ref[...], w_ref[...]).astype(BF16)


def _inproj(x, nw, w, wba, tm, tn):
    rows, n = x.shape[0], w.shape[1]
    return pl.pallas_call(
        _inproj_kernel,
        grid=(rows // tm, n // tn),
        in_specs=[
            pl.BlockSpec((tm, D_MODEL), lambda i, j: (i, 0)),
            pl.BlockSpec((1, D_MODEL), lambda i, j: (0, 0)),
            pl.BlockSpec((D_MODEL, tn), lambda i, j: (0, j)),
            pl.BlockSpec((D_MODEL, 256), lambda i, j: (0, 0)),
        ],
        out_specs=[
            pl.BlockSpec((tm, tn), lambda i, j: (i, j)),
            pl.BlockSpec((tm, 256), lambda i, j: (i, 0)),
        ],
        out_shape=[
            jax.ShapeDtypeStruct((rows, n), BF16),
            jax.ShapeDtypeStruct((rows, 256), F32),
        ],
        scratch_shapes=[pltpu.VMEM((tm, D_MODEL), BF16)],
        compiler_params=pltpu.CompilerParams(
            dimension_semantics=("parallel", "arbitrary"), vmem_limit_bytes=VMEM_LIMIT),
        name="inproj",
    )(x, nw, w, wba)


def _tap_groups(n_taps):
    hb = BF16_SUBLANES
    r = n_taps // 2
    return [((hb - r + s) // hb, (hb - r + s + CHUNK - 1) // hb - (hb - r + s) // hb + 1) for s in range(n_taps)]


def _shift_matrix(n_taps):
    hb = BF16_SUBLANES
    r = n_taps // 2
    groups = _tap_groups(n_taps)
    k = sum(n for _, n in groups) * hb
    ri = lax.broadcasted_iota(jnp.int32, (CHUNK, k), 0)
    ci = lax.broadcasted_iota(jnp.int32, (CHUNK, k), 1)
    hit = jnp.zeros((CHUNK, k), jnp.bool_)
    base = 0
    for s, (first, n) in enumerate(groups):
        hit = jnp.logical_or(hit, ci == ri + (base + hb - r + s - first * hb))
        base += n * hb
    return hit, ri


def _weighted_taps(x, cw_ref, n_taps):
    taps = []
    for s, (first, n) in enumerate(_tap_groups(n_taps)):
        taps.append((x[first:first + n] * cw_ref[s][None]).reshape(n * BF16_SUBLANES, x.shape[-1]))
    return jnp.concatenate(taps, axis=0)


def _prep_kernel(nt, tm, t_pad, pm_ref, pp_ref, pn_ref, cw_ref, o_ref, xb_ref):
    i = pl.program_id(0)
    hb = BF16_SUBLANES
    width = 3 * D_MODEL
    xb_ref[0:hb, :] = pp_ref[...]
    xb_ref[hb:hb + tm, :] = pm_ref[...]
    xb_ref[hb + tm:, :] = jnp.where(i == nt - 1, jnp.zeros((), BF16), pn_ref[...])

    win = CHUNK + 2 * hb
    shift, ri = _shift_matrix(DN_CONV)
    for blk in range(tm // CHUNK):
        x = xb_ref[blk * CHUNK:blk * CHUNK + win, :].reshape(win // hb, hb, width)
        taps = _weighted_taps(x, cw_ref, DN_CONV)
        pos = lax.rem(i * tm + blk * CHUNK, t_pad) + ri
        pos = jnp.where(pos >= t_pad, pos - t_pad, pos)
        shift_v = jnp.where(jnp.logical_and(shift, pos >= PAD_FRONT), 1.0, 0.0).astype(BF16)
        qkv = _silu(_dot(shift_v, taps))

        rows = slice(blk * CHUNK, (blk + 1) * CHUNK)
        for h in range(HEADS):
            for part in range(3):
                sl = slice(part * D_MODEL + h * HEAD_DIM, part * D_MODEL + (h + 1) * HEAD_DIM)
                t = qkv[:, sl]
                if part == 0:
                    t = t * (lax.rsqrt(jnp.sum(t * t, axis=-1, keepdims=True) + L2_EPS) * (HEAD_DIM ** -0.5))
                elif part == 1:
                    t = t * lax.rsqrt(jnp.sum(t * t, axis=-1, keepdims=True) + L2_EPS)
                o_ref[rows, sl] = t.astype(BF16)


def _prep(p, cw, tm, t_pad):
    rows = p.shape[0]
    nt = rows // tm
    hb = BF16_SUBLANES
    per = tm // hb
    n_halo = rows // hb
    width = 3 * D_MODEL
    return pl.pallas_call(
        functools.partial(_prep_kernel, nt, tm, t_pad),
        grid=(nt,),
        in_specs=[
            pl.BlockSpec((tm, width), lambda i: (i, 0)),
            pl.BlockSpec((hb, width), lambda i: (jnp.maximum(i * per - 1, 0), 0)),
            pl.BlockSpec((hb, width), lambda i: (jnp.minimum(i * per + per, n_halo - 1), 0)),
            pl.BlockSpec((DN_CONV, hb, width), lambda i: (0, 0, 0)),
        ],
        out_specs=pl.BlockSpec((tm, width), lambda i: (i, 0)),
        out_shape=jax.ShapeDtypeStruct((rows, width), BF16),
        scratch_shapes=[pltpu.VMEM((tm + 2 * hb, width), BF16)],
        compiler_params=pltpu.CompilerParams(
            dimension_semantics=("parallel",), vmem_limit_bytes=VMEM_LIMIT),
        name="qkvprep",
    )(p, p, p, cw)


def _dn_kernel(nc, qf_ref, qb_ref, baf_ref, bab_ref, gp_ref, of_ref, ob_ref, s_ref):
    n = pl.program_id(1)

    @pl.when(n == 0)
    def _():
        s_ref[...] = jnp.zeros_like(s_ref)

    rid = lax.broadcasted_iota(jnp.int32, (CHUNK, 1), 0)
    ri = lax.broadcasted_iota(jnp.int32, (CHUNK, CHUNK), 0)
    ci = lax.broadcasted_iota(jnp.int32, (CHUNK, CHUNK), 1)
    lane = lax.broadcasted_iota(jnp.int32, (1, 128), 1)
    a_lanes = jnp.logical_and(lane >= HEADS, lane < 2 * HEADS)
    dirs = []
    for bi in range(DN_BATCH):
        dirs.append((2 * bi, qf_ref, baf_ref, of_ref, n, bi, 0))
        dirs.append((2 * bi + 1, qb_ref, bab_ref, ob_ref, nc - 1 - n, bi, 1))
    n_streams = len(dirs)

    gate = {}
    for d, _, ba_ref, _, c, bi, dirn in dirs:
        ba = ba_ref[bi]
        valid = jnp.logical_or(c > 0, rid >= PAD_FRONT).astype(F32)
        neg_a = jnp.where(a_lanes, -jnp.exp(gp_ref[dirn, 0:1, :]), 0.0)
        g = neg_a * jax.nn.softplus(ba + gp_ref[dirn, 1:2, :]) * valid
        tri = (ri >= ci) if dirn == 0 else (ri <= ci)
        g_hi = g.astype(BF16)
        r1 = g - g_hi.astype(F32)
        g_mid = r1.astype(BF16)
        g_lo = (r1 - g_mid.astype(F32)).astype(BF16)
        tri_b = tri.astype(BF16)
        gc = _dot(jnp.concatenate([tri_b, tri_b, tri_b], axis=1), jnp.concatenate([g_hi, g_mid, g_lo], axis=0))
        gl = jnp.sum(g, axis=0, keepdims=True)
        beta = jax.nn.sigmoid(ba)
        gate[d] = dict(beta=beta, beta_t=beta.T, gc=gc, gc_t=gc.T, eg=jnp.exp(gc), egl=jnp.exp(gl - gc),
                       eg_last=jnp.exp(gl))

    heads = [(d, h) for d in range(n_streams) for h in range(HEADS)]
    pairs = [(d, t) for d in range(n_streams) for t in range(HEADS // 2)]
    pw_lanes = 2 * CHUNK

    def head(d, part, h, n_heads=1):
        lo = part * D_MODEL + h * HEAD_DIM
        return dirs[d][1][dirs[d][5], :, lo:lo + n_heads * HEAD_DIM]

    def col(d, name, lane_idx):
        return gate[d][name][:, lane_idx:lane_idx + 1]

    def pair_row(d, name, base, t):
        g_t = gate[d][name]
        return jnp.concatenate([g_t[base + 2 * t:base + 2 * t + 1, :], g_t[base + 2 * t + 1:base + 2 * t + 2, :]], axis=1)

    lane_p = lax.broadcasted_iota(jnp.int32, (CHUNK, pw_lanes), 1)
    row_p = lax.broadcasted_iota(jnp.int32, (CHUNK, pw_lanes), 0)
    first_half = lane_p < CHUNK
    col_p = jnp.where(first_half, lane_p, lane_p - CHUNK)
    tri_p = {0: row_p >= col_p, 1: row_p <= col_p}
    strict_p = {0: row_p > col_p, 1: row_p < col_p}
    r2 = lax.broadcasted_iota(jnp.int32, (2 * CHUNK, pw_lanes), 0)
    c2 = lax.broadcasted_iota(jnp.int32, (2 * CHUNK, pw_lanes), 1)
    bd_p = (r2 >= CHUNK) == (c2 >= CHUNK)
    r3 = lax.broadcasted_iota(jnp.int32, (2 * CHUNK, 2 * HEAD_DIM), 0)
    c3 = lax.broadcasted_iota(jnp.int32, (2 * CHUNK, 2 * HEAD_DIM), 1)
    bd_k = (r3 >= CHUNK) == (c3 >= HEAD_DIM)

    def block_diag(pair):
        return jnp.where(bd_p, jnp.concatenate([pair, pair], axis=0), jnp.zeros((), pair.dtype))

    kkqk = {}
    for d, t in pairs:
        k2 = head(d, 1, 2 * t, 2)
        q2 = head(d, 0, 2 * t, 2)
        k_bd = jnp.where(bd_k, jnp.concatenate([k2, k2], axis=0), jnp.zeros((), BF16))
        kkqk[d, t] = lax.dot_general(jnp.concatenate([k2, q2], axis=0), k_bd, (((1,), (1,)), ((), ())),
                                     preferred_element_type=F32)

    decay = {}
    for d, t in pairs:
        ga = HEADS + 2 * t
        g_col = jnp.where(first_half, col(d, "gc", ga), col(d, "gc", ga + 1))
        decay[d, t] = jnp.exp(jnp.where(tri_p[dirs[d][6]], g_col - pair_row(d, "gc_t", HEADS, t), -jnp.inf))
    beta_row = {(d, t): pair_row(d, "beta_t", 0, t) for d, t in pairs}

    eye_p = (row_p == col_p).astype(F32)
    same_block = (row_p // INV_BLOCK) == (col_p // INV_BLOCK)

    def pair_dot(a, b):
        return _dot(a.astype(BF16), block_diag(b.astype(BF16)))

    t_inv = {}
    pw = {}
    n_out = {}
    for i in pairs:
        nm = jnp.where(strict_p[dirs[i[0]][6]], -(kkqk[i][:CHUNK] * decay[i] * beta_row[i]), 0.0)
        n_in = jnp.where(same_block, nm, 0.0)
        n_out[i] = nm - n_in
        t_inv[i] = n_in + eye_p
        pw[i] = pair_dot(n_in, n_in)
    assert CHUNK == 4 * INV_BLOCK
    n_sq = INV_BLOCK.bit_length() - 3
    for j in range(n_sq + 1):
        if j < n_sq:
            x2 = {}
            for i in pairs:
                pb = pw[i].astype(BF16)
                x2[i] = _dot(jnp.concatenate([pb, t_inv[i].astype(BF16)], axis=0), block_diag(pb))
            t_inv = {i: t_inv[i] + x2[i][CHUNK:] for i in pairs}
            pw = {i: x2[i][:CHUNK] for i in pairs}
        else:
            t_inv = {i: t_inv[i] + pair_dot(t_inv[i], pw[i]) for i in pairs}
    m1 = {i: pair_dot(t_inv[i], n_out[i]) for i in pairs}
    m2 = {i: pair_dot(m1[i], m1[i]) for i in pairs}
    g1 = {i: m1[i] + eye_p for i in pairs}
    g2 = {i: g1[i] + pair_dot(g1[i], m2[i]) for i in pairs}
    inv = {i: pair_dot(g2[i], t_inv[i]) for i in pairs}

    sol = {}
    for d, t in pairs:
        rhs = []
        for h in (2 * t, 2 * t + 1):
            k_eg = (head(d, 1, h).astype(F32) * col(d, "eg", HEADS + h)).astype(BF16)
            rhs.append(jnp.concatenate([head(d, 2, h), k_eg], axis=1))
        both = _dot(block_diag(inv[d, t].astype(BF16)), jnp.concatenate(rhs, axis=0))
        sol[d, 2 * t] = both[:CHUNK]
        sol[d, 2 * t + 1] = both[CHUNK:]

    s_old = {(d, h): s_ref[dirs[d][5], dirs[d][6], h] for d, h in heads}
    ws_qs = {}
    for d, h in heads:
        qd = (head(d, 0, h).astype(F32) * col(d, "eg", HEADS + h)).astype(BF16)
        ws_qs[d, h] = _dot(jnp.concatenate([sol[d, h][:, HEAD_DIM:].astype(BF16), qd], axis=0), s_old[d, h].astype(BF16))
    v_hat = {i: (sol[i][:, :HEAD_DIM] - ws_qs[i][:CHUNK]).astype(BF16) for i in heads}
    intra = {}
    for d, t in pairs:
        qkd = (kkqk[d, t][CHUNK:] * decay[d, t] * beta_row[d, t]).astype(BF16)
        both = _dot(block_diag(qkd), jnp.concatenate([v_hat[d, 2 * t], v_hat[d, 2 * t + 1]], axis=0))
        intra[d, 2 * t] = both[:CHUNK]
        intra[d, 2 * t + 1] = both[CHUNK:]
    for d, h in heads:
        ga = HEADS + h
        kd = (head(d, 1, h).astype(F32) * (col(d, "egl", ga) * col(d, "beta", h))).astype(BF16)
        s_ref[dirs[d][5], dirs[d][6], h] = s_old[d, h] * gate[d]["eg_last"][:, ga:ga + 1] + lax.dot_general(
            kd, v_hat[d, h], (((0,), (0,)), ((), ())), preferred_element_type=F32)
        dirs[d][3][dirs[d][5], :, h * HEAD_DIM:(h + 1) * HEAD_DIM] = (ws_qs[d, h][CHUNK:] + intra[d, h]).astype(BF16)


def _deltanet(qkv, ba, gp, nb, nc):
    t_pad = qkv.shape[1]
    width = 3 * D_MODEL
    out = jax.ShapeDtypeStruct((nb, t_pad, D_MODEL), BF16)
    return pl.pallas_call(
        functools.partial(_dn_kernel, nc),
        grid=(nb // DN_BATCH, nc),
        in_specs=[
            pl.BlockSpec((DN_BATCH, CHUNK, width), lambda b, n: (b, n, 0)),
            pl.BlockSpec((DN_BATCH, CHUNK, width), lambda b, n: (b, nc - 1 - n, 0)),
            pl.BlockSpec((DN_BATCH, CHUNK, 128), lambda b, n: (b, n, 0)),
            pl.BlockSpec((DN_BATCH, CHUNK, 128), lambda b, n: (b, nc - 1 - n, 1)),
            pl.BlockSpec((2, 8, 128), lambda b, n: (0, 0, 0)),
        ],
        out_specs=[
            pl.BlockSpec((DN_BATCH, CHUNK, D_MODEL), lambda b, n: (b, n, 0)),
            pl.BlockSpec((DN_BATCH, CHUNK, D_MODEL), lambda b, n: (b, nc - 1 - n, 0)),
        ],
        out_shape=[out, out],
        scratch_shapes=[pltpu.VMEM((DN_BATCH, 2, HEADS, HEAD_DIM, HEAD_DIM), F32)],
        compiler_params=pltpu.CompilerParams(
            dimension_semantics=("parallel", "arbitrary"), vmem_limit_bytes=VMEM_LIMIT),
        name="deltanet",
    )(qkv, qkv, ba, ba, gp)


def _merge_kernel(nt, tm, of_ref, ob_ref, z_ref, scb_ref, scc_ref, scx_ref, sccp_ref, scxp_ref, sccn_ref, scxn_ref,
                  ga_ref, gb_ref, h_ref, nw_ref, cw_ref, wdn_ref, wsc_ref, wout_ref, out_ref, xb_ref):
    is_last = pl.program_id(0) == nt - 1

    hb = BF16_SUBLANES
    xb_ref[0:hb, :] = sccp_ref[...] * scxp_ref[...]
    xb_ref[hb:hb + tm, :] = scc_ref[...] * scx_ref[...]
    xb_ref[hb + tm:, :] = jnp.where(is_last, jnp.zeros((), BF16), sccn_ref[...] * scxn_ref[...])
    win = CHUNK + 2 * hb
    shift = jnp.where(_shift_matrix(SC_CONV)[0], 1.0, 0.0).astype(BF16)

    gated = []
    for blk in range(tm // CHUNK):
        x = xb_ref[blk * CHUNK:blk * CHUNK + win, :].reshape(win // hb, hb, D_MODEL)
        conv = _dot(shift, _weighted_taps(x, cw_ref, SC_CONV))
        gated.append(scb_ref[blk * CHUNK:(blk + 1) * CHUNK, :] * conv.astype(BF16))
    y_b = _dot(jnp.concatenate(gated, axis=0), wsc_ref[...])
    mix_b = jax.nn.sigmoid(gb_ref[...].astype(F32)) * y_b

    y_a = jnp.zeros((tm, D_MODEL), F32)
    for lo in range(0, D_MODEL, 2 * HEAD_DIM):
        cols = slice(lo, lo + 2 * HEAD_DIM)
        o = of_ref[:, cols].astype(F32) + ob_ref[:, cols].astype(F32)
        z = z_ref[:, cols].astype(F32)
        parts = []
        for h0 in (0, HEAD_DIM):
            sl = slice(h0, h0 + HEAD_DIM)
            parts.append((_rms(o[:, sl], nw_ref[...]) * _silu(z[:, sl])).astype(BF16))
        y_a = y_a + _dot(jnp.concatenate(parts, axis=1), wdn_ref[cols, :])

    merged = jax.nn.sigmoid(ga_ref[...].astype(F32)) * y_a + mix_b
    out_ref[...] = h_ref[...] + _dot(merged.astype(BF16), wout_ref[...])


def _merge(o_f, o_b, p, hres, nw, cw, wdn, wsc, wout, tm):
    rows = hres.shape[0]
    nt = rows // tm
    hb = BF16_SUBLANES
    per = tm // hb
    n_halo = rows // hb

    def col(c):
        return pl.BlockSpec((tm, D_MODEL), lambda i: (i, c))

    def prev(c):
        return pl.BlockSpec((hb, D_MODEL), lambda i: (jnp.maximum(i * per - 1, 0), c))

    def nxt(c):
        return pl.BlockSpec((hb, D_MODEL), lambda i: (jnp.minimum(i * per + per, n_halo - 1), c))

    def full(shape):
        return pl.BlockSpec(shape, lambda i: (0,) * len(shape))

    return pl.pallas_call(
        functools.partial(_merge_kernel, nt, tm),
        grid=(nt,),
        in_specs=[
            col(0), col(0),
            col(3), col(4), col(5), col(6), prev(5), prev(6), nxt(5), nxt(6), col(7), col(8),
            pl.BlockSpec((tm, D_MODEL), lambda i: (i, 0)),
            full((1, HEAD_DIM)), full((SC_CONV, hb, D_MODEL)),
            full((D_MODEL, D_MODEL)), full((D_MODEL, D_MODEL)), full((D_MODEL, D_MODEL)),
        ],
        out_specs=pl.BlockSpec((tm, D_MODEL), lambda i: (i, 0)),
        out_shape=jax.ShapeDtypeStruct((rows, D_MODEL), F32),
        scratch_shapes=[pltpu.VMEM((tm + 2 * hb, D_MODEL), BF16)],
        input_output_aliases={12: 0},
        compiler_params=pltpu.CompilerParams(
            dimension_semantics=("parallel",), vmem_limit_bytes=VMEM_LIMIT),
        name="merge",
    )(o_f, o_b, p, p, p, p, p, p, p, p, p, p, hres, nw, cw, wdn, wsc, wout)


def _mlp_kernel(final, x_ref, nw_ref, wg_ref, wu_ref, wd_ref, fw_ref, out_ref):
    x = x_ref[...].reshape(x_ref.shape[-2:])
    h2 = _rms(x, nw_ref[...]).astype(BF16)
    acc = x
    for lo in range(0, D_FF, FF_CHUNK):
        sl = slice(lo, lo + FF_CHUNK)
        a = _silu(_dot(h2, wg_ref[:, sl])) * _dot(h2, wu_ref[:, sl])
        acc = acc + _dot(a.astype(BF16), wd_ref[sl, :])
    if final:
        acc = _rms(acc, fw_ref[...])
    out_ref[...] = acc


def _mlp(x, nw, wg, wu, wd, fw, tm):
    rows = x.shape[0]

    def full(shape):
        return pl.BlockSpec(shape, lambda i: (0,) * len(shape), pipeline_mode=pl.Buffered(1))

    return pl.pallas_call(
        functools.partial(_mlp_kernel, False),
        grid=(rows // tm,),
        in_specs=[
            pl.BlockSpec((tm, D_MODEL), lambda i: (i, 0)),
            full((1, D_MODEL)), full((D_MODEL, D_FF)), full((D_MODEL, D_FF)), full((D_FF, D_MODEL)),
            full((1, D_MODEL)),
        ],
        out_specs=pl.BlockSpec((tm, D_MODEL), lambda i: (i, 0)),
        out_shape=jax.ShapeDtypeStruct((rows, D_MODEL), F32),
        input_output_aliases={0: 0},
        compiler_params=pltpu.CompilerParams(
            dimension_semantics=("parallel",), vmem_limit_bytes=VMEM_LIMIT),
        name="mlp",
    )(x, nw, wg, wu, wd, fw)


def _mlp_final(x, nw, wg, wu, wd, fw, nb, seq, tm):
    t_pad = x.shape[0] // nb
    front = t_pad - seq

    def full(shape):
        return pl.BlockSpec(shape, lambda b, j: (0,) * len(shape))

    return pl.pallas_call(
        functools.partial(_mlp_kernel, True),
        grid=(nb, seq // tm),
        in_specs=[
            pl.BlockSpec((pl.Element(tm), pl.Element(D_MODEL)),
                         lambda b, j: (pl.multiple_of(b * t_pad + front + j * tm, 8), 0)),
            full((1, D_MODEL)), full((D_MODEL, D_FF)), full((D_MODEL, D_FF)), full((D_FF, D_MODEL)),
            full((1, D_MODEL)),
        ],
        out_specs=pl.BlockSpec((None, tm, D_MODEL), lambda b, j: (b, j, 0)),
        out_shape=jax.ShapeDtypeStruct((nb, seq, D_MODEL), F32),
        compiler_params=pltpu.CompilerParams(
            dimension_semantics=("parallel", "parallel"), vmem_limit_bytes=VMEM_LIMIT),
        name="mlp_final",
    )(x, nw, wg, wu, wd, fw)


def _row_tile(rows, want, step=BF16_SUBLANES):
    t = want - want % step
    while rows % t:
        t -= step
    return t


def kernel(x, meta_tokens, norm1_w, w_in, dn_conv_w, A_log, dt_bias, dn_norm_w, sc_conv_w, w_branch_dn,
           w_branch_sc, w_out, norm2_w, w_gate_up, w_down, final_norm_w):
    nb, seq, _ = x.shape
    depth = w_in.shape[0]
    t_pad = PAD_FRONT + N_META + seq
    nc = t_pad // CHUNK
    rows = nb * t_pad

    meta = jnp.broadcast_to(meta_tokens[None].astype(x.dtype), (nb, N_META, D_MODEL))
    hres = jnp.concatenate([jnp.zeros((nb, PAD_FRONT, D_MODEL), x.dtype), meta, x], axis=1)
    hres = hres.reshape(rows, D_MODEL)

    q_cols = 4 * D_MODEL
    ba_cols = 4 * HEADS
    tm_in = _row_tile(rows, 1536)
    tm_prep = _row_tile(rows, 768, CHUNK)
    tm_merge = _row_tile(rows, 512, CHUNK)
    tm_mlp = _row_tile(rows, 768)

    for l in range(depth):
        w_main = jnp.concatenate([w_in[l, :, :q_cols], w_in[l, :, q_cols + ba_cols:]], axis=1).astype(BF16)
        w_b = w_in[l, :, q_cols:q_cols + 2 * HEADS]
        w_a = w_in[l, :, q_cols + 2 * HEADS:q_cols + ba_cols]
        zpad = jnp.zeros((D_MODEL, 128 - 2 * HEADS), w_in.dtype)
        w_ba = jnp.concatenate([w_b[:, :HEADS], w_a[:, :HEADS], zpad,
                                w_b[:, HEADS:], w_a[:, HEADS:], zpad], axis=1).astype(BF16)
        p, ba = _inproj(hres, norm1_w[l][None], w_main, w_ba, tm_in, 1536)

        cw = jnp.broadcast_to(dn_conv_w[l].astype(BF16)[:, None, :], (DN_CONV, BF16_SUBLANES, 3 * D_MODEL))
        qkv = _prep(p, cw, tm_prep, t_pad)
        lane_pad = ((0, 0), (HEADS, 128 - 2 * HEADS))
        gp = jnp.stack([jnp.pad(A_log[l], lane_pad), jnp.pad(dt_bias[l], lane_pad)], axis=1)
        gp = jnp.pad(gp, ((0, 0), (0, 6), (0, 0))).astype(F32)
        o_f, o_b = _deltanet(qkv.reshape(nb, t_pad, 3 * D_MODEL), ba.reshape(nb, t_pad, 256), gp, nb, nc)
        o_f = o_f.reshape(rows, D_MODEL)
        o_b = o_b.reshape(rows, D_MODEL)

        sc_cw = jnp.broadcast_to(sc_conv_w[l].astype(BF16)[:, None, :], (SC_CONV, BF16_SUBLANES, D_MODEL))
        hres = _merge(o_f, o_b, p, hres, dn_norm_w[l][None], sc_cw,
                      w_branch_dn[l].astype(BF16), w_branch_sc[l].astype(BF16), w_out[l].astype(BF16), tm_merge)

        mlp_w = (norm2_w[l][None], w_gate_up[l, :, :D_FF].astype(BF16), w_gate_up[l, :, D_FF:].astype(BF16),
                 w_down[l].astype(BF16), final_norm_w[None])
        if l < depth - 1:
            hres = _mlp(hres, *mlp_w, tm_mlp)
        else:
            out = _mlp_final(hres, *mlp_w, nb, seq, _row_tile(seq, 512))
    return out
```
